```python
import math
import jax, jax.numpy as jnp
from jax import lax
import numpy as np

D_MODEL = 1024
BATCH = 32
SEQ = 2048
DEPTH = 1

CHUNK = 64
Q_BLOCK = 128
PLE_DIM = 256
GDN_HEADS = 4
GDN_DK = 128
GDN_DV = 128
GDN_CONV = 4
MLA_HEADS = 4
MLA_NOPE = 128
MLA_ROPE = 64
MLA_V = 128
MLA_Q_LORA = 384
MLA_KV_LORA = 256
ROPE_THETA = 10000.0
D_FF = 2816
FFN_CONV = 3
ALPHA = (2.0 * DEPTH) ** 0.25
BETA = (8.0 * DEPTH) ** -0.25
NORM_EPS = 1e-6
GDN_QK = GDN_HEADS * GDN_DK
GDN_VW = GDN_HEADS * GDN_DV
D_IN = 2 * GDN_QK + 2 * GDN_VW + 2 * GDN_HEADS + MLA_Q_LORA + MLA_KV_LORA + MLA_ROPE
D_MIX = GDN_VW + MLA_HEADS * MLA_V

kernel_name = "hybrid_gdn_mla_convffn_deepnorm"


def _rmsnorm(x, g):
    xf = x.astype(jnp.float32)
    y = xf * lax.rsqrt(jnp.mean(xf * xf, axis=-1, keepdims=True) + NORM_EPS)
    return (y * g.astype(jnp.float32)).astype(x.dtype)


def _layernorm(x, g, b):
    xf = x.astype(jnp.float32)
    mu = jnp.mean(xf, axis=-1, keepdims=True)
    xc = xf - mu
    var = jnp.mean(xc * xc, axis=-1, keepdims=True)
    y = xc * lax.rsqrt(var + NORM_EPS) * g.astype(jnp.float32) + b.astype(jnp.float32)
    return y.astype(x.dtype)


def _l2norm(x):
    return x * lax.rsqrt(jnp.sum(x * x, axis=-1, keepdims=True) + NORM_EPS)


def _causal_dwconv(x, w):
    k = w.shape[0]
    return lax.conv_general_dilated(
        x, w[:, None, :].astype(x.dtype), window_strides=(1,), padding=[(k - 1, 0)],
        dimension_numbers=("NWC", "WIO", "NWC"), feature_group_count=x.shape[-1])


def _rope_tables(seq):
    inv = ROPE_THETA ** (-jnp.arange(0, MLA_ROPE, 2, dtype=jnp.float32) / MLA_ROPE)
    ang = jnp.arange(seq, dtype=jnp.float32)[:, None] * inv[None, :]
    return jnp.cos(ang), jnp.sin(ang)


def _apply_rope(x, cos, sin):
    xf = x.astype(jnp.float32)
    x1, x2 = jnp.split(xf, 2, axis=-1)
    return jnp.concatenate([x1 * cos - x2 * sin, x1 * sin + x2 * cos], axis=-1).astype(x.dtype)


def _chunk_gated_delta_rule(q, k, v, g, beta):
    bsz, seq, nh, dk = q.shape
    dv = v.shape[-1]
    n = seq // CHUNK

    def blocks(t):
        t = t.reshape((bsz, n, CHUNK, nh) + t.shape[3:])
        return jnp.moveaxis(jnp.swapaxes(t, 2, 3), 1, 0)

    qc, kc, vc = blocks(q), blocks(k), blocks(v)
    bc = blocks(beta)
    gc = jnp.cumsum(blocks(g), axis=-1)
    idx = jnp.arange(CHUNK)
    incl = idx[:, None] >= idx[None, :]
    strict = idx[:, None] > idx[None, :]
    decay = jnp.exp(jnp.where(incl, gc[..., :, None] - gc[..., None, :], -jnp.inf))
    kb = kc * bc[..., None]
    lower = jnp.where(strict, jnp.einsum("nbhcd,nbhsd->nbhcs", kb, kc) * decay, 0.0)
    tri = lower + jnp.eye(CHUNK, dtype=jnp.float32)
    w = lax.linalg.triangular_solve(tri, kb * jnp.exp(gc)[..., None], left_side=True, lower=True,
                                    unit_diagonal=True)
    u = lax.linalg.triangular_solve(tri, vc * bc[..., None], left_side=True, lower=True,
                                    unit_diagonal=True)
    qk = jnp.einsum("nbhcd,nbhsd->nbhcs", qc, kc) * decay
    qg = qc * jnp.exp(gc)[..., None]
    kd = kc * jnp.exp(gc[..., -1:] - gc)[..., None]
    glast = jnp.exp(gc[..., -1])

    def step(state, xs):
        w_n, u_n, qk_n, qg_n, kd_n, gl_n = xs
        v_new = u_n - jnp.einsum("bhck,bhkv->bhcv", w_n, state)
        o_n = jnp.einsum("bhck,bhkv->bhcv", qg_n, state) + jnp.einsum("bhcs,bhsv->bhcv", qk_n, v_new)
        state = state * gl_n[..., None, None] + jnp.einsum("bhck,bhcv->bhkv", kd_n, v_new)
        return state, o_n

    s0 = jnp.zeros((bsz, nh, dk, dv), jnp.float32)
    _, o = lax.scan(step, s0, (w, u, qk, qg, kd, glast))
    return jnp.swapaxes(jnp.moveaxis(o, 0, 1), 2, 3).reshape(bsz, seq, nh, dv)


def _gated_deltanet(qkv, z, a, b, conv_w, a_log, dt_bias, norm_g):
    bsz, seq, _ = qkv.shape
    h = jax.nn.silu(_causal_dwconv(qkv, conv_w)).astype(jnp.float32)
    q, k, v = jnp.split(h, [GDN_QK, 2 * GDN_QK], axis=-1)
    q = _l2norm(q.reshape(bsz, seq, GDN_HEADS, GDN_DK)) * (GDN_DK ** -0.5)
    k = _l2norm(k.reshape(bsz, seq, GDN_HEADS, GDN_DK))
    v = v.reshape(bsz, seq, GDN_HEADS, GDN_DV)
    beta = jax.nn.sigmoid(b.astype(jnp.float32))
    g = -jnp.exp(a_log.astype(jnp.float32)) * jax.nn.softplus(
        a.astype(jnp.float32) + dt_bias.astype(jnp.float32))
    o = _chunk_gated_delta_rule(q, k, v, g, beta)
    o = _rmsnorm(o, norm_g) * jax.nn.silu(z.astype(jnp.float32).reshape(bsz, seq, GDN_HEADS, GDN_DV))
    return o.reshape(bsz, seq, GDN_VW).astype(z.dtype)


def _mla(cq, ckv, k_rope, q_norm_g, w_q_up, kv_norm_g, w_kv_up):
    bsz, seq, _ = cq.shape
    q = (_rmsnorm(cq, q_norm_g) @ w_q_up).reshape(bsz, seq, MLA_HEADS, MLA_NOPE + MLA_ROPE)
    q_nope, q_rope = q[..., :MLA_NOPE], q[..., MLA_NOPE:]
    kv = (_rmsnorm(ckv, kv_norm_g) @ w_kv_up).reshape(bsz, seq, MLA_HEADS, MLA_NOPE + MLA_V)
    k_nope, v = kv[..., :MLA_NOPE], kv[..., MLA_NOPE:]
    cos, sin = _rope_tables(seq)
    q_rope = _apply_rope(q_rope, cos[:, None, :], sin[:, None, :])
    k_rope = _apply_rope(k_rope, cos, sin)
    nqb = seq // Q_BLOCK

    def qblocks(t):
        return jnp.moveaxis(t.reshape(bsz, nqb, Q_BLOCK, MLA_HEADS, t.shape[-1]), 1, 0)

    key_chunk = jnp.arange(seq) // CHUNK
    scale = (MLA_NOPE + MLA_ROPE) ** -0.5

    def attend(xs):
        qn, qr, blk = xs
        s = jnp.einsum("bqhd,bkhd->bhqk", qn, k_nope) + jnp.einsum("bqhd,bkd->bhqk", qr, k_rope)
        q_chunk = (blk * Q_BLOCK + jnp.arange(Q_BLOCK)) // CHUNK
        allowed = key_chunk[None, :] <= q_chunk[:, None]
        s = jnp.where(allowed, s.astype(jnp.float32) * scale, -jnp.inf)
        pr = jax.nn.softmax(s, axis=-1).astype(v.dtype)
        return jnp.einsum("bhqk,bkhd->bqhd", pr, v)

    o = lax.map(attend, (qblocks(q_nope), qblocks(q_rope), jnp.arange(nqb)))
    return jnp.moveaxis(o, 0, 1).reshape(bsz, seq, MLA_HEADS * MLA_V)


def _conv_ffn(h, w_up, conv_w, conv_b, w_down):
    u = _causal_dwconv(h @ w_up, conv_w) + conv_b
    gate, up = jnp.split(u, 2, axis=-1)
    return (jax.nn.silu(gate) * up) @ w_down


def setup_inputs(seed: int = 0) -> dict:
    key = jax.random.key(seed)
    ks = jax.random.split(key, 24)
    f32 = jnp.float32
    L = DEPTH

    def nrm(k, shape, scale):
        return jax.random.normal(k, shape, f32) * scale

    dt = jnp.exp(jax.random.uniform(ks[5], (L, GDN_HEADS), f32, math.log(1e-3), math.log(1e-1)))
    return {
        "x": nrm(ks[0], (BATCH, SEQ, D_MODEL), 1.0),
        "p": nrm(ks[1], (L, BATCH, SEQ, PLE_DIM), 1.0),
        "w_in": nrm(ks[2], (L, D_MODEL, D_IN), D_MODEL ** -0.5),
        "gdn_conv_w": nrm(ks[3], (L, GDN_CONV, 2 * GDN_QK + GDN_VW), GDN_CONV ** -0.5),
        "gdn_a_log": jnp.log(jax.random.uniform(ks[4], (L, GDN_HEADS), f32, 1.0, 16.0)),
        "gdn_dt_bias": dt + jnp.log(-jnp.expm1(-dt)),
        "gdn_norm_g": 1.0 + nrm(ks[6], (L, GDN_DV), 0.02),
        "mla_q_norm_g": 1.0 + nrm(ks[7], (L, MLA_Q_LORA), 0.02),
        "mla_w_q_up": nrm(ks[8], (L, MLA_Q_LORA, MLA_HEADS * (MLA_NOPE + MLA_ROPE)), MLA_Q_LORA ** -0.5),
        "mla_kv_norm_g": 1.0 + nrm(ks[9], (L, MLA_KV_LORA), 0.02),
        "mla_w_kv_up": nrm(ks[10], (L, MLA_KV_LORA, MLA_HEADS * (MLA_NOPE + MLA_V)), MLA_KV_LORA ** -0.5),
        "w_out": nrm(ks[11], (L, D_MIX, D_MODEL), BETA * D_MIX ** -0.5),
        "ln1_g": 1.0 + nrm(ks[12], (L, D_MODEL), 0.02),
        "ln1_b": nrm(ks[13], (L, D_MODEL), 0.02),
        "ffn_w_up": nrm(ks[14], (L, D_MODEL, 2 * D_FF), D_MODEL ** -0.5),
        "ffn_conv_w": nrm(ks[15], (L, FFN_CONV, 2 * D_FF), FFN_CONV ** -0.5),
        "ffn_conv_b": nrm(ks[16], (L, 2 * D_FF), 0.01),
        "ffn_w_down": nrm(ks[17], (L, D_FF, D_MODEL), BETA * D_FF ** -0.5),
        "ple_w_gate": nrm(ks[18], (L, D_MODEL, D_MODEL), D_MODEL ** -0.5),
        "ple_b_gate": nrm(ks[19], (L, D_MODEL), 0.01),
        "ple_w_proj": nrm(ks[20], (L, PLE_DIM, D_MODEL), BETA * PLE_DIM ** -0.5),
        "ln2_g": 1.0 + nrm(ks[21], (L, D_MODEL), 0.02),
        "ln2_b": nrm(ks[22], (L, D_MODEL), 0.02),
    }


def reference(x, p, w_in, gdn_conv_w, gdn_a_log, gdn_dt_bias, gdn_norm_g, mla_q_norm_g, mla_w_q_up,
              mla_kv_norm_g, mla_w_kv_up, w_out, ln1_g, ln1_b, ffn_w_up, ffn_conv_w, ffn_conv_b,
              ffn_w_down, ple_w_gate, ple_b_gate, ple_w_proj, ln2_g, ln2_b):
    o_qkv = 2 * GDN_QK + GDN_VW
    o_z = o_qkv + GDN_VW
    o_a = o_z + GDN_HEADS
    o_b = o_a + GDN_HEADS
    o_cq = o_b + MLA_Q_LORA
    o_ckv = o_cq + MLA_KV_LORA
    h = x
    for i in range(DEPTH):
        proj = h @ w_in[i]
        qkv, z, a, b, cq, ckv, k_rope = jnp.split(proj, [o_qkv, o_z, o_a, o_b, o_cq, o_ckv], axis=-1)
        out_a = _gated_deltanet(qkv, z, a, b, gdn_conv_w[i], gdn_a_log[i], gdn_dt_bias[i], gdn_norm_g[i])
        out_b = _mla(cq, ckv, k_rope, mla_q_norm_g[i], mla_w_q_up[i], mla_kv_norm_g[i], mla_w_kv_up[i])
        mix = jnp.concatenate([out_a, out_b], axis=-1) @ w_out[i]
        h = _layernorm(ALPHA * h + mix, ln1_g[i], ln1_b[i])
        ffn = _conv_ffn(h, ffn_w_up[i], ffn_conv_w[i], ffn_conv_b[i], ffn_w_down[i])
        ple = jax.nn.sigmoid(h @ ple_w_gate[i] + ple_b_gate[i]) * (p[i] @ ple_w_proj[i])
        h = _layernorm(ALPHA * h + ffn + ple, ln2_g[i], ln2_b[i])
    return h
```

```python
import functools
import math

import jax
import jax.numpy as jnp
from jax import lax
from jax.experimental import pallas as pl
from jax.experimental.pallas import tpu as pltpu

F32 = jnp.float32
BF16 = jnp.bfloat16

D_MODEL = 1024
CHUNK = 64
PLE_DIM = 256
GDN_HEADS = 4
GDN_DK = 128
GDN_DV = 128
GDN_CONV = 4
MLA_HEADS = 4
MLA_NOPE = 128
MLA_ROPE = 64
MLA_V = 128
MLA_Q_LORA = 384
MLA_KV_LORA = 256
ROPE_THETA = 10000.0
D_FF = 2816
FFN_CONV = 3
DEPTH = 1
ALPHA = (2.0 * DEPTH) ** 0.25
NORM_EPS = 1e-6
GDN_QK = GDN_HEADS * GDN_DK
GDN_VW = GDN_HEADS * GDN_DV
MLA_QKW = MLA_NOPE + MLA_ROPE

LANES = 128
SUBLANES = 8
VMEM_LIMIT_BYTES = 56 * 1024 * 1024

PROJ_ROWS = 512
GDN_CHUNKS = 4
ATTN_ROWS = 512
FFN_ROWS = 512
FFN_COLS = 256

C_QKV = 0
C_Z = C_QKV + 2 * GDN_QK + GDN_VW
C_CQ = C_Z + GDN_VW
C_CKV = C_CQ + MLA_Q_LORA
C_KR = C_CKV + MLA_KV_LORA
C_AB = C_KR + 2 * MLA_ROPE
C_END = C_AB + LANES


def _dot(a, b):
    return jnp.dot(a, b, preferred_element_type=F32)


def _dot_nt(a, b):
    return lax.dot_general(a, b, (((1,), (1,)), ((), ())), preferred_element_type=F32)


def _dot_tn(a, b):
    return lax.dot_general(a, b, (((0,), (0,)), ((), ())), preferred_element_type=F32)


def _sigmoid(x):
    return 1.0 / (1.0 + jnp.exp(-x))


def _silu(x):
    return x * _sigmoid(x)


def _softplus(x):
    return jnp.maximum(x, 0.0) + jnp.log1p(jnp.exp(-jnp.abs(x)))


def _layernorm(x, g, b):
    mu = jnp.mean(x, axis=-1, keepdims=True)
    xc = x - mu
    var = jnp.mean(xc * xc, axis=-1, keepdims=True)
    return xc * lax.rsqrt(var + NORM_EPS) * g + b


def _rmsnorm(x, g):
    return x * lax.rsqrt(jnp.mean(x * x, axis=-1, keepdims=True) + NORM_EPS) * g


def _causal_conv(ext, w_ref, cols, taps):
    y = ext * w_ref[taps - 1:taps, cols]
    for d in range(1, taps):
        y = y + pltpu.roll(ext, d, 0) * w_ref[taps - 1 - d:taps - d, cols]
    return y[SUBLANES:]


def _proj_kernel(x_ref, w1_ref, convw_ref, gpar_ref, qng_ref, wq_ref, kvng_ref, wkv_ref, rope_ref,
                 gq_ref, gk_ref, gv_ref, gz_ref, gcol_ref, mq_ref, mk_ref, mv_ref, carry_ref):
    rows = x_ref.shape[0]

    @pl.when(pl.program_id(1) == 0)
    def _():
        carry_ref[...] = jnp.zeros_like(carry_ref)

    xb = x_ref[...].astype(BF16)

    outs = (gq_ref, gk_ref, gv_ref)
    for part in range(3):
        for h in range(GDN_HEADS):
            cols = slice(part * GDN_QK + h * GDN_DK, part * GDN_QK + (h + 1) * GDN_DK)
            pre = _dot(xb, w1_ref[:, cols])
            ext = jnp.concatenate([carry_ref[:, cols], pre], axis=0)
            carry_ref[:, cols] = pre[rows - SUBLANES:]
            y = _silu(_causal_conv(ext, convw_ref, cols, GDN_CONV))
            if part < 2:
                y = y * lax.rsqrt(jnp.sum(y * y, axis=-1, keepdims=True) + NORM_EPS)
            if part == 0:
                y = y * (GDN_DK ** -0.5)
            outs[part][:, h * GDN_DK:(h + 1) * GDN_DK] = y.astype(BF16)

    gz_ref[...] = _dot(xb, w1_ref[:, C_Z:C_Z + GDN_VW])

    ab = _dot(xb, w1_ref[:, C_AB:C_AB + LANES])
    lane = lax.broadcasted_iota(jnp.int32, ab.shape, 1)
    rpos = lax.broadcasted_iota(jnp.int32, ab.shape, 0) & (CHUNK - 1)
    cs = -jnp.exp(gpar_ref[0:1, :]) * _softplus(ab + gpar_ref[1:2, :])
    sh = 1
    while sh < CHUNK:
        cs = cs + jnp.where(rpos >= sh, pltpu.roll(cs, sh, 0), 0.0)
        sh *= 2
    gcol_ref[...] = jnp.where(lane < GDN_HEADS, cs, _sigmoid(ab))

    tab = rope_ref[...]
    half = lax.broadcasted_iota(jnp.int32, tab.shape, 1) < MLA_ROPE

    def rope(r):
        t = r * tab
        return jnp.where(half, t + pltpu.roll(t, MLA_ROPE, 1), 0.0)

    krope = rope(_dot(xb, w1_ref[:, C_KR:C_KR + LANES]))[:, :MLA_ROPE].astype(BF16)
    cq = _rmsnorm(_dot(xb, w1_ref[:, C_CQ:C_CQ + MLA_Q_LORA]), qng_ref[...]).astype(BF16)
    ckv = _rmsnorm(_dot(xb, w1_ref[:, C_CKV:C_CKV + MLA_KV_LORA]), kvng_ref[...]).astype(BF16)
    for h in range(MLA_HEADS):
        qh = _dot(cq, wq_ref[:, h * 2 * LANES:(h + 1) * 2 * LANES])
        qr = rope(qh[:, LANES:])[:, :MLA_ROPE]
        mq_ref[h] = jnp.concatenate([qh[:, :LANES], qr], axis=-1).astype(BF16)
        kn = _dot(ckv, wkv_ref[:, h * MLA_NOPE:(h + 1) * MLA_NOPE]).astype(BF16)
        mk_ref[h] = jnp.concatenate([kn, krope], axis=-1)
        vo = MLA_HEADS * MLA_NOPE + h * MLA_V
        mv_ref[h] = _dot(ckv, wkv_ref[:, vo:vo + MLA_V]).astype(BF16)


def _gdn_kernel(q_ref, k_ref, v_ref, gcol_ref, z_ref, ng_ref, o_ref, state_ref):
    @pl.when(pl.program_id(1) == 0)
    def _():
        state_ref[...] = jnp.zeros_like(state_ref)

    c = CHUNK
    gcol_t = gcol_ref[...].T
    ri = lax.broadcasted_iota(jnp.int32, (c, c), 0)
    ci = lax.broadcasted_iota(jnp.int32, (c, c), 1)
    incl = ri >= ci
    strict = ri > ci
    eye = jnp.where(ri == ci, 1.0, 0.0).astype(F32)
    ng = ng_ref[...]

    for n in range(GDN_CHUNKS):
        rs = slice(n * c, (n + 1) * c)
        gb = gcol_ref[rs, :]
        for h in range(GDN_HEADS):
            hs = slice(h * GDN_DK, (h + 1) * GDN_DK)
            gc_c = gb[:, h:h + 1]
            beta = gb[:, GDN_HEADS + h:GDN_HEADS + h + 1]
            gc_r = gcol_t[h:h + 1, rs]
            g_last = gcol_t[h:h + 1, (n + 1) * c - 1:(n + 1) * c]
            decay = jnp.where(incl, jnp.exp(gc_c - gc_r), 0.0)
            eg = jnp.exp(gc_c)
            q = q_ref[rs, hs]
            k = k_ref[rs, hs]
            kf = k.astype(F32)
            kb = kf * beta
            kk = _dot_nt(jnp.concatenate([kb.astype(BF16), q], axis=0), k)
            neg_a = jnp.where(strict, -(kk[:c] * decay), 0.0)
            qk = kk[c:] * decay
            inv = eye + neg_a
            npow = neg_a
            for _ in range(5):
                nb = npow.astype(BF16)
                npow = _dot(nb, nb)
                inv = inv + _dot(inv.astype(BF16), npow.astype(BF16))
            rhs = jnp.concatenate([kb * eg, v_ref[rs, hs].astype(F32) * beta], axis=-1)
            wu = _dot(inv.astype(BF16), rhs.astype(BF16))
            state = state_ref[h]
            lhs = jnp.concatenate([wu[:, :GDN_DK], q.astype(F32) * eg], axis=0)
            ws_qs = _dot(lhs.astype(BF16), state.astype(BF16))
            v_new = (wu[:, GDN_DK:] - ws_qs[:c]).astype(BF16)
            kd = (kf * jnp.exp(g_last - gc_c)).astype(BF16)
            o = ws_qs[c:] + _dot(qk.astype(BF16), v_new)
            state_ref[h] = state * jnp.exp(g_last) + _dot_tn(kd, v_new)
            o = o * lax.rsqrt(jnp.mean(o * o, axis=-1, keepdims=True) + NORM_EPS) * ng
            o_ref[rs, hs] = (o * _silu(z_ref[rs, hs])).astype(BF16)


def _attn_kernel(q_ref, k_ref, v_ref, o_ref):
    t = ATTN_ROWS
    qi = pl.program_id(2)
    q = q_ref[...]
    scale = MLA_QKW ** -0.5

    def block(kb, carry, masked):
        m, l, acc = carry
        start = pl.multiple_of(kb * t, t)
        s = _dot_nt(q, k_ref[pl.ds(start, t), :]) * scale
        if masked:
            qc = lax.broadcasted_iota(jnp.int32, s.shape, 0) // CHUNK
            kc = lax.broadcasted_iota(jnp.int32, s.shape, 1) // CHUNK
            s = jnp.where(kc <= qc, s, -jnp.inf)
        m_new = jnp.maximum(m, jnp.max(s, axis=-1, keepdims=True))
        a = jnp.exp(m - m_new)
        p = jnp.exp(s - m_new)
        l = a * l + jnp.sum(p, axis=-1, keepdims=True)
        acc = a * acc + _dot(p.astype(BF16), v_ref[pl.ds(start, t), :])
        return m_new, l, acc

    init = (jnp.full((t, 1), -jnp.inf, F32), jnp.zeros((t, 1), F32), jnp.zeros((t, MLA_V), F32))
    carry = lax.fori_loop(0, qi, lambda kb, cr: block(kb, cr, False), init)
    _, l, acc = block(qi, carry, True)
    o_ref[...] = (acc / l).astype(BF16)


def _ffn_kernel(x_ref, oa_ref, ob_ref, p_ref, wo_ref, ln1g_ref, ln1b_ref, wup_ref, cw_ref, cb_ref,
                wdn_ref, wg_ref, bg_ref, wp_ref, ln2g_ref, ln2b_ref, out_ref, carry_ref, acc_ref):
    rows = x_ref.shape[0]
    nf = wup_ref.shape[0]

    @pl.when(pl.program_id(1) == 0)
    def _():
        carry_ref[...] = jnp.zeros_like(carry_ref)

    mix = _dot(oa_ref[...], wo_ref[:GDN_VW, :]) + _dot(ob_ref[...], wo_ref[GDN_VW:, :])
    h = _layernorm(ALPHA * x_ref[...] + mix, ln1g_ref[...], ln1b_ref[...])
    hb = h.astype(BF16)
    ple = _sigmoid(_dot(hb, wg_ref[...]) + bg_ref[...]) * _dot(p_ref[...].astype(BF16), wp_ref[...])
    acc_ref[...] = ALPHA * h + ple

    def body(f, _):
        u = _dot(hb, wup_ref[f])
        ext = jnp.concatenate([carry_ref[f], u], axis=0)
        carry_ref[f] = u[rows - SUBLANES:]
        y = _causal_conv(ext, cw_ref.at[f], slice(None), FFN_CONV) + cb_ref[f]
        act = _silu(y[:, :FFN_COLS]) * y[:, FFN_COLS:]
        acc_ref[...] += _dot(act.astype(BF16), wdn_ref[f])
        return 0

    lax.fori_loop(0, nf, body, 0)
    out_ref[...] = _layernorm(acc_ref[...], ln2g_ref[...], ln2b_ref[...])


def _const_spec(shape):
    nd = len(shape)
    return pl.BlockSpec(shape, lambda *_: (0,) * nd, pipeline_mode=pl.Buffered(1))


def _params(*sem):
    return pltpu.CompilerParams(dimension_semantics=sem, vmem_limit_bytes=VMEM_LIMIT_BYTES)


def _swap_halves(w):
    half = w.shape[-1] // 2
    return jnp.concatenate([w[..., half:], w[..., :half]], axis=-1)


def _rope_table(seq):
    inv = ROPE_THETA ** (-jnp.arange(0, MLA_ROPE, 2, dtype=F32) / MLA_ROPE)
    ang = jnp.arange(seq, dtype=F32)[:, None] * inv[None, :]
    cos, sin = jnp.cos(ang), jnp.sin(ang)
    return jnp.concatenate([cos, cos, -sin, sin], axis=-1)


def _layer(x, p, w_in, gdn_conv_w, gdn_a_log, gdn_dt_bias, gdn_norm_g, mla_q_norm_g, mla_w_q_up,
           mla_kv_norm_g, mla_w_kv_up, w_out, ln1_g, ln1_b, ffn_w_up, ffn_conv_w, ffn_conv_b,
           ffn_w_down, ple_w_gate, ple_b_gate, ple_w_proj, ln2_g, ln2_b):
    bsz, seq, _ = x.shape
    assert seq % PROJ_ROWS == 0 and seq % (GDN_CHUNKS * CHUNK) == 0
    assert seq % ATTN_ROWS == 0 and seq % FFN_ROWS == 0 and D_FF % FFN_COLS == 0

    o_z = 2 * GDN_QK + GDN_VW
    o_a = o_z + GDN_VW
    o_cq = o_a + 2 * GDN_HEADS
    o_ckv = o_cq + MLA_Q_LORA
    o_kr = o_ckv + MLA_KV_LORA
    w_kr = w_in[:, o_kr:o_kr + MLA_ROPE]
    w1 = jnp.concatenate([
        w_in[:, :o_a], w_in[:, o_cq:o_kr], w_kr, _swap_halves(w_kr), w_in[:, o_a:o_cq],
        jnp.zeros((D_MODEL, LANES - 2 * GDN_HEADS), F32)], axis=1).astype(BF16)
    assert w1.shape[1] == C_END
    gpar = jnp.zeros((2, LANES), F32)
    gpar = gpar.at[0, :GDN_HEADS].set(gdn_a_log).at[1, :GDN_HEADS].set(gdn_dt_bias)
    wq = mla_w_q_up.reshape(MLA_Q_LORA, MLA_HEADS, MLA_QKW)
    wq = jnp.concatenate([wq, _swap_halves(wq[..., MLA_NOPE:])], axis=-1)
    wq = wq.reshape(MLA_Q_LORA, MLA_HEADS * 2 * LANES).astype(BF16)
    wkv = mla_w_kv_up.reshape(MLA_KV_LORA, MLA_HEADS, MLA_NOPE + MLA_V)
    wkv = jnp.concatenate([wkv[..., :MLA_NOPE].reshape(MLA_KV_LORA, -1),
                           wkv[..., MLA_NOPE:].reshape(MLA_KV_LORA, -1)], axis=1).astype(BF16)
    rope_tab = _rope_table(seq)

    nf = D_FF // FFN_COLS

    def ffn_cols(a):
        g = a[..., :D_FF].reshape(a.shape[:-1] + (nf, FFN_COLS))
        u = a[..., D_FF:].reshape(a.shape[:-1] + (nf, FFN_COLS))
        return jnp.moveaxis(jnp.concatenate([g, u], axis=-1), -2, 0)

    wup = ffn_cols(ffn_w_up).astype(BF16)
    cw = ffn_cols(ffn_conv_w)
    cb = ffn_cols(ffn_conv_b[None, :])
    wdn = ffn_w_down.reshape(nf, FFN_COLS, D_MODEL).astype(BF16)

    row = lambda v: v.reshape(1, -1)

    tm = PROJ_ROWS
    tok = lambda n: pl.BlockSpec((None, tm, n), lambda b, s: (b, s, 0))
    hd = lambda n: pl.BlockSpec((None, MLA_HEADS, tm, n), lambda b, s: (b, 0, s, 0))
    sd = jax.ShapeDtypeStruct
    gq, gk, gv, gz, gcol, mq, mk, mv = pl.pallas_call(
        _proj_kernel,
        grid=(bsz, seq // tm),
        in_specs=[tok(D_MODEL), _const_spec(w1.shape), _const_spec(gdn_conv_w.shape),
                  _const_spec(gpar.shape), _const_spec((1, MLA_Q_LORA)), _const_spec(wq.shape),
                  _const_spec((1, MLA_KV_LORA)), _const_spec(wkv.shape),
                  pl.BlockSpec((tm, LANES), lambda b, s: (s, 0))],
        out_specs=[tok(GDN_QK), tok(GDN_QK), tok(GDN_VW), tok(GDN_VW), tok(LANES),
                   hd(MLA_QKW), hd(MLA_QKW), hd(MLA_V)],
        out_shape=[sd((bsz, seq, GDN_QK), BF16), sd((bsz, seq, GDN_QK), BF16),
                   sd((bsz, seq, GDN_VW), BF16), sd((bsz, seq, GDN_VW), F32),
                   sd((bsz, seq, LANES), F32),
                   sd((bsz, MLA_HEADS, seq, MLA_QKW), BF16), sd((bsz, MLA_HEADS, seq, MLA_QKW), BF16),
                   sd((bsz, MLA_HEADS, seq, MLA_V), BF16)],
        scratch_shapes=[pltpu.VMEM((SUBLANES, 2 * GDN_QK + GDN_VW), F32)],
        compiler_params=_params("parallel", "arbitrary"),
        name="proj",
    )(x, w1, gdn_conv_w, gpar, row(mla_q_norm_g), wq, row(mla_kv_norm_g), wkv, rope_tab)

    cs = GDN_CHUNKS * CHUNK
    gtok = lambda n: pl.BlockSpec((None, cs, n), lambda b, s: (b, s, 0))
    out_a = pl.pallas_call(
        _gdn_kernel,
        grid=(bsz, seq // cs),
        in_specs=[gtok(GDN_QK), gtok(GDN_QK), gtok(GDN_VW), gtok(LANES), gtok(GDN_VW),
                  _const_spec((1, GDN_DV))],
        out_specs=gtok(GDN_VW),
        out_shape=sd((bsz, seq, GDN_VW), BF16),
        scratch_shapes=[pltpu.VMEM((GDN_HEADS, GDN_DK, GDN_DV), F32)],
        compiler_params=_params("parallel", "arbitrary"),
        name="gdn",
    )(gq, gk, gv, gcol, gz, row(gdn_norm_g))

    tq = ATTN_ROWS
    out_b = pl.pallas_call(
        _attn_kernel,
        grid=(bsz, MLA_HEADS, seq // tq),
        in_specs=[pl.BlockSpec((None, None, tq, MLA_QKW), lambda b, h, i: (b, h, i, 0)),
                  pl.BlockSpec((None, None, seq, MLA_QKW), lambda b, h, i: (b, h, 0, 0)),
                  pl.BlockSpec((None, None, seq, MLA_V), lambda b, h, i: (b, h, 0, 0))],
        out_specs=pl.BlockSpec((None, tq, MLA_V), lambda b, h, i: (b, i, h)),
        out_shape=sd((bsz, seq, MLA_HEADS * MLA_V), BF16),
        compiler_params=_params("parallel", "parallel", "arbitrary"),
        name="attn",
    )(mq, mk, mv)

    tf = FFN_ROWS
    ftok = lambda n: pl.BlockSpec((None, tf, n), lambda b, s: (b, s, 0))
    out = pl.pallas_call(
        _ffn_kernel,
        grid=(bsz, seq // tf),
        in_specs=[ftok(D_MODEL), ftok(GDN_VW), ftok(MLA_HEADS * MLA_V), ftok(PLE_DIM),
                  _const_spec((GDN_VW + MLA_HEADS * MLA_V, D_MODEL)),
                  _const_spec((1, D_MODEL)), _const_spec((1, D_MODEL)),
                  _const_spec(wup.shape), _const_spec(cw.shape), _const_spec(cb.shape),
                  _const_spec(wdn.shape), _const_spec((D_MODEL, D_MODEL)), _const_spec((1, D_MODEL)),
                  _const_spec((PLE_DIM, D_MODEL)), _const_spec((1, D_MODEL)), _const_spec((1, D_MODEL))],
        out_specs=ftok(D_MODEL),
        out_shape=sd((bsz, seq, D_MODEL), F32),
        scratch_shapes=[pltpu.VMEM((nf, SUBLANES, 2 * FFN_COLS), F32),
                        pltpu.VMEM((tf, D_MODEL), F32)],
        compiler_params=_params("parallel", "arbitrary"),
        name="ffn",
    )(x, out_a, out_b, p, w_out.astype(BF16), row(ln1_g), row(ln1_b), wup, cw, cb, wdn,
      ple_w_gate.astype(BF16), row(ple_b_gate), ple_w_proj.astype(BF16), row(ln2_g), row(ln2_b))
    return out


def kernel(x, p, w_in, gdn_conv_w, gdn_a_log, gdn_dt_bias, gdn_norm_g, mla_q_norm_g, mla_w_q_up,
           mla_kv_norm_g, mla_w_kv_up, w_out, ln1_g, ln1_b, ffn_w_up, ffn_conv_w, ffn_conv_b,
           ffn_w_down, ple_w_gate, ple_b_gate, ple_w_proj, ln2_g, ln2_b):
    h = x
    for i in range(DEPTH):
        h = _layer(h, p[i], w_in[i], gdn_conv_w[i], gdn_a_log[i], gdn_dt_bias[i], gdn_norm_g[i],
                   mla_q_norm_g[i], mla_w_q_up[i], mla_kv_norm_g[i], mla_w_kv_up[i], w_out[i],
                   ln1_g[i], ln1_b[i], ffn_w_up[i], ffn_conv_w[i], ffn_conv_b[i], ffn_w_down[i],
                   ple_w_gate[i], ple_b_gate[i], ple_w_proj[i], ln2_g[i], ln2_b[i])
    return h
```

```python
import functools
import math

import jax
import jax.numpy as jnp
from jax import lax
from jax.experimental import pallas as pl
from jax.experimental.pallas import tpu as pltpu

F32 = jnp.float32
BF16 = jnp.bfloat16

D_MODEL = 1024
CHUNK = 64
PLE_DIM = 256
GDN_HEADS = 4
GDN_DK = 128
GDN_DV = 128
GDN_CONV = 4
MLA_HEADS = 4
MLA_NOPE = 128
MLA_ROPE = 64
MLA_V = 128
MLA_Q_LORA = 384
MLA_KV_LORA = 256
ROPE_THETA = 10000.0
D_FF = 2816
FFN_CONV = 3
DEPTH = 1
ALPHA = (2.0 * DEPTH) ** 0.25
NORM_EPS = 1e-6
GDN_QK = GDN_HEADS * GDN_DK
GDN_VW = GDN_HEADS * GDN_DV
MLA_QKW = MLA_NOPE + MLA_ROPE

LANES = 128
SUBLANES = 8
VMEM_LIMIT_BYTES = 56 * 1024 * 1024

PROJ_ROWS = 512
GDN_CHUNKS = 8
ATTN_ROWS = 512
FFN_ROWS = 512
FFN_COLS = 256

C_QKV = 0
C_Z = C_QKV + 2 * GDN_QK + GDN_VW
C_CQ = C_Z + GDN_VW
C_CKV = C_CQ + MLA_Q_LORA
C_KR = C_CKV + MLA_KV_LORA
C_AB = C_KR + 2 * MLA_ROPE
C_END = C_AB + LANES


def _dot(a, b):
    return jnp.dot(a, b, preferred_element_type=F32)


def _dot_nt(a, b):
    return lax.dot_general(a, b, (((1,), (1,)), ((), ())), preferred_element_type=F32)


def _dot_tn(a, b):
    return lax.dot_general(a, b, (((0,), (0,)), ((), ())), preferred_element_type=F32)


def _sigmoid(x):
    return 1.0 / (1.0 + jnp.exp(-x))


def _silu(x):
    return x * _sigmoid(x)


def _softplus(x):
    return jnp.maximum(x, 0.0) + jnp.log1p(jnp.exp(-jnp.abs(x)))


def _layernorm(x, g, b):
    mu = jnp.mean(x, axis=-1, keepdims=True)
    xc = x - mu
    var = jnp.mean(xc * xc, axis=-1, keepdims=True)
    return xc * lax.rsqrt(var + NORM_EPS) * g + b


def _rmsnorm(x, g):
    return x * lax.rsqrt(jnp.mean(x * x, axis=-1, keepdims=True) + NORM_EPS) * g


def _causal_conv(ext, w_ref, cols, taps):
    y = ext * w_ref[taps - 1:taps, cols]
    for d in range(1, taps):
        y = y + pltpu.roll(ext, d, 0) * w_ref[taps - 1 - d:taps - d, cols]
    return y[SUBLANES:]


def _proj_kernel(x_ref, w1_ref, convw_ref, gpar_ref, qng_ref, wq_ref, kvng_ref, wkv_ref, rope_ref,
                 gq_ref, gk_ref, gv_ref, gz_ref, gcol_ref, mq_ref, mk_ref, mv_ref, carry_ref):
    rows = x_ref.shape[0]

    @pl.when(pl.program_id(1) == 0)
    def _():
        carry_ref[...] = jnp.zeros_like(carry_ref)

    xb = x_ref[...].astype(BF16)

    outs = (gq_ref, gk_ref, gv_ref)
    for part in range(3):
        for h in range(GDN_HEADS):
            cols = slice(part * GDN_QK + h * GDN_DK, part * GDN_QK + (h + 1) * GDN_DK)
            pre = _dot(xb, w1_ref[:, cols])
            ext = jnp.concatenate([carry_ref[:, cols], pre], axis=0)
            carry_ref[:, cols] = pre[rows - SUBLANES:]
            y = _silu(_causal_conv(ext, convw_ref, cols, GDN_CONV))
            if part < 2:
                y = y * lax.rsqrt(jnp.sum(y * y, axis=-1, keepdims=True) + NORM_EPS)
            if part == 0:
                y = y * (GDN_DK ** -0.5)
            outs[part][:, h * GDN_DK:(h + 1) * GDN_DK] = y.astype(BF16)

    gz_ref[...] = _dot(xb, w1_ref[:, C_Z:C_Z + GDN_VW])

    ab = _dot(xb, w1_ref[:, C_AB:C_AB + LANES])
    lane = lax.broadcasted_iota(jnp.int32, ab.shape, 1)
    rpos = lax.broadcasted_iota(jnp.int32, ab.shape, 0) & (CHUNK - 1)
    cs = -jnp.exp(gpar_ref[0:1, :]) * _softplus(ab + gpar_ref[1:2, :])
    sh = 1
    while sh < CHUNK:
        cs = cs + jnp.where(rpos >= sh, pltpu.roll(cs, sh, 0), 0.0)
        sh *= 2
    gcol_ref[...] = jnp.where(lane < GDN_HEADS, cs, _sigmoid(ab))

    tab = rope_ref[...]
    half = lax.broadcasted_iota(jnp.int32, tab.shape, 1) < MLA_ROPE

    def rope(r):
        t = r * tab
        return jnp.where(half, t + pltpu.roll(t, MLA_ROPE, 1), 0.0)

    krope = rope(_dot(xb, w1_ref[:, C_KR:C_KR + LANES]))[:, :MLA_ROPE].astype(BF16)
    cq = _rmsnorm(_dot(xb, w1_ref[:, C_CQ:C_CQ + MLA_Q_LORA]), qng_ref[...]).astype(BF16)
    ckv = _rmsnorm(_dot(xb, w1_ref[:, C_CKV:C_CKV + MLA_KV_LORA]), kvng_ref[...]).astype(BF16)
    for h in range(MLA_HEADS):
        qh = _dot(cq, wq_ref[:, h * 2 * LANES:(h + 1) * 2 * LANES])
        qr = rope(qh[:, LANES:])[:, :MLA_ROPE]
        mq_ref[h] = jnp.concatenate([qh[:, :LANES], qr], axis=-1).astype(BF16)
        kn = _dot(ckv, wkv_ref[:, h * MLA_NOPE:(h + 1) * MLA_NOPE]).astype(BF16)
        mk_ref[h] = jnp.concatenate([kn, krope], axis=-1)
        vo = MLA_HEADS * MLA_NOPE + h * MLA_V
        mv_ref[h] = _dot(ckv, wkv_ref[:, vo:vo + MLA_V]).astype(BF16)


def _gdn_kernel(q_ref, k_ref, v_ref, gcol_ref, z_ref, ng_ref, o_ref, state_ref):
    @pl.when(pl.program_id(1) == 0)
    def _():
        state_ref[...] = jnp.zeros_like(state_ref)

    c = CHUNK
    nh = GDN_HEADS
    nc = GDN_CHUNKS
    ng_all = nc * nh
    gcol_t = gcol_ref[...].T
    ri = lax.broadcasted_iota(jnp.int32, (ng_all, c, c), 1)
    ci = lax.broadcasted_iota(jnp.int32, (ng_all, c, c), 2)
    incl = ri >= ci
    strict = ri > ci
    eye = jnp.where(ri == ci, 1.0, 0.0).astype(F32)
    bmm = functools.partial(jnp.einsum, preferred_element_type=F32)

    def problems(fn):
        return jnp.stack([fn(slice(n * c, (n + 1) * c), h) for n in range(nc) for h in range(nh)])

    def heads(ref):
        return problems(lambda rs, h: ref[rs, h * GDN_DK:(h + 1) * GDN_DK])

    gc_c = problems(lambda rs, h: gcol_ref[rs, h:h + 1])
    beta = problems(lambda rs, h: gcol_ref[rs, nh + h:nh + h + 1])
    gc_r = problems(lambda rs, h: gcol_t[h:h + 1, rs])
    g_last = gc_r[:, :, c - 1:c]
    decay = jnp.where(incl, jnp.exp(gc_c - gc_r), 0.0)
    eg = jnp.exp(gc_c)
    q = heads(q_ref)
    k = heads(k_ref)
    kf = k.astype(F32)
    kb = kf * beta
    kk = bmm("gid,gjd->gij", jnp.concatenate([kb.astype(BF16), q], axis=1), k)
    neg_a = jnp.where(strict, -(kk[:, :c] * decay), 0.0)
    qk = (kk[:, c:] * decay).astype(BF16)
    mp = jnp.concatenate([neg_a, eye], axis=-1)
    right = lax.broadcasted_iota(jnp.int32, mp.shape, 2) >= c
    level = 1
    while level < c:
        mp = bmm("gij,gjk->gik", mp[:, :, :c].astype(BF16), mp.astype(BF16)) + jnp.where(right, mp, 0.0)
        level *= 2
    inv = mp[:, :, c:]
    rhs = jnp.concatenate([kb * eg, heads(v_ref).astype(F32) * beta], axis=-1)
    wu = bmm("gij,gjd->gid", inv.astype(BF16), rhs.astype(BF16))
    kd_t = jnp.swapaxes(kf * jnp.exp(g_last - gc_c), 1, 2).astype(BF16)
    lhs2 = jnp.concatenate([qk, kd_t], axis=1)
    lhs1 = jnp.concatenate([wu[:, :, :GDN_DK], q.astype(F32) * eg], axis=1).astype(BF16)
    u = wu[:, :, GDN_DK:]
    g_decay = jnp.exp(g_last)

    state = state_ref[...]
    outs = []
    for n in range(nc):
        gs = slice(n * nh, (n + 1) * nh)
        ws_qs = bmm("hid,hdv->hiv", lhs1[gs], state.astype(BF16))
        v_new = (u[gs] - ws_qs[:, :c]).astype(BF16)
        upd = bmm("hij,hjv->hiv", lhs2[gs], v_new)
        state = state * g_decay[gs] + upd[:, c:]
        o = ws_qs[:, c:] + upd[:, :c]
        outs.append(jnp.concatenate([o[h] for h in range(nh)], axis=-1))
    state_ref[...] = state

    o = jnp.concatenate(outs, axis=0)
    z = z_ref[...]
    for h in range(nh):
        hs = slice(h * GDN_DV, (h + 1) * GDN_DV)
        oh = o[:, hs]
        oh = oh * lax.rsqrt(jnp.mean(oh * oh, axis=-1, keepdims=True) + NORM_EPS) * ng_ref[...]
        o_ref[:, hs] = (oh * _silu(z[:, hs])).astype(BF16)


def _attn_kernel(q_ref, k_ref, v_ref, o_ref):
    t = ATTN_ROWS
    qi = pl.program_id(2)
    q = q_ref[...]
    scale = MLA_QKW ** -0.5

    def block(kb, carry, masked):
        m, l, acc = carry
        start = pl.multiple_of(kb * t, t)
        s = _dot_nt(q, k_ref[pl.ds(start, t), :]) * scale
        if masked:
            qc = lax.broadcasted_iota(jnp.int32, s.shape, 0) // CHUNK
            kc = lax.broadcasted_iota(jnp.int32, s.shape, 1) // CHUNK
            s = jnp.where(kc <= qc, s, -jnp.inf)
        m_new = jnp.maximum(m, jnp.max(s, axis=-1, keepdims=True))
        a = jnp.exp(m - m_new)
        p = jnp.exp(s - m_new)
        l = a * l + jnp.sum(p, axis=-1, keepdims=True)
        acc = a * acc + _dot(p.astype(BF16), v_ref[pl.ds(start, t), :])
        return m_new, l, acc

    init = (jnp.full((t, 1), -jnp.inf, F32), jnp.zeros((t, 1), F32), jnp.zeros((t, MLA_V), F32))
    carry = lax.fori_loop(0, qi, lambda kb, cr: block(kb, cr, False), init)
    _, l, acc = block(qi, carry, True)
    o_ref[...] = (acc / l).astype(BF16)


def _ffn_kernel(x_ref, oa_ref, ob_ref, p_ref, wo_ref, ln1g_ref, ln1b_ref, wup_ref, cw_ref, cb_ref,
                wdn_ref, wg_ref, bg_ref, wp_ref, ln2g_ref, ln2b_ref, out_ref, carry_ref, acc_ref):
    rows = x_ref.shape[0]
    nf = wup_ref.shape[0]

    @pl.when(pl.program_id(1) == 0)
    def _():
        carry_ref[...] = jnp.zeros_like(carry_ref)

    mix = _dot(oa_ref[...], wo_ref[:GDN_VW, :]) + _dot(ob_ref[...], wo_ref[GDN_VW:, :])
    h = _layernorm(ALPHA * x_ref[...] + mix, ln1g_ref[...], ln1b_ref[...])
    hb = h.astype(BF16)
    ple = _sigmoid(_dot(hb, wg_ref[...]) + bg_ref[...]) * _dot(p_ref[...].astype(BF16), wp_ref[...])
    acc_ref[...] = ALPHA * h + ple

    u = _dot(hb, wup_ref[0])
    for f in range(nf):
        u_next = _dot(hb, wup_ref[f + 1]) if f + 1 < nf else None
        ext = jnp.concatenate([carry_ref[f], u], axis=0)
        carry_ref[f] = u[rows - SUBLANES:]
        y = _causal_conv(ext, cw_ref.at[f], slice(None), FFN_CONV) + cb_ref[f]
        act = _silu(y[:, :FFN_COLS]) * y[:, FFN_COLS:]
        acc_ref[...] += _dot(act.astype(BF16), wdn_ref[f])
        u = u_next
    out_ref[...] = _layernorm(acc_ref[...], ln2g_ref[...], ln2b_ref[...])


def _const_spec(shape):
    nd = len(shape)
    return pl.BlockSpec(shape, lambda *_: (0,) * nd, pipeline_mode=pl.Buffered(1))


def _params(*sem):
    return pltpu.CompilerParams(dimension_semantics=sem, vmem_limit_bytes=VMEM_LIMIT_BYTES)


def _swap_halves(w):
    half = w.shape[-1] // 2
    return jnp.concatenate([w[..., half:], w[..., :half]], axis=-1)


def _rope_table(seq):
    inv = ROPE_THETA ** (-jnp.arange(0, MLA_ROPE, 2, dtype=F32) / MLA_ROPE)
    ang = jnp.arange(seq, dtype=F32)[:, None] * inv[None, :]
    cos, sin = jnp.cos(ang), jnp.sin(ang)
    return jnp.concatenate([cos, cos, -sin, sin], axis=-1)


def _layer(x, p, w_in, gdn_conv_w, gdn_a_log, gdn_dt_bias, gdn_norm_g, mla_q_norm_g, mla_w_q_up,
           mla_kv_norm_g, mla_w_kv_up, w_out, ln1_g, ln1_b, ffn_w_up, ffn_conv_w, ffn_conv_b,
           ffn_w_down, ple_w_gate, ple_b_gate, ple_w_proj, ln2_g, ln2_b):
    bsz, seq, _ = x.shape
    assert seq % PROJ_ROWS == 0 and seq % (GDN_CHUNKS * CHUNK) == 0
    assert seq % ATTN_ROWS == 0 and seq % FFN_ROWS == 0 and D_FF % FFN_COLS == 0

    o_z = 2 * GDN_QK + GDN_VW
    o_a = o_z + GDN_VW
    o_cq = o_a + 2 * GDN_HEADS
    o_ckv = o_cq + MLA_Q_LORA
    o_kr = o_ckv + MLA_KV_LORA
    w_kr = w_in[:, o_kr:o_kr + MLA_ROPE]
    w1 = jnp.concatenate([
        w_in[:, :o_a], w_in[:, o_cq:o_kr], w_kr, _swap_halves(w_kr), w_in[:, o_a:o_cq],
        jnp.zeros((D_MODEL, LANES - 2 * GDN_HEADS), F32)], axis=1).astype(BF16)
    assert w1.shape[1] == C_END
    gpar = jnp.zeros((2, LANES), F32)
    gpar = gpar.at[0, :GDN_HEADS].set(gdn_a_log).at[1, :GDN_HEADS].set(gdn_dt_bias)
    wq = mla_w_q_up.reshape(MLA_Q_LORA, MLA_HEADS, MLA_QKW)
    wq = jnp.concatenate([wq, _swap_halves(wq[..., MLA_NOPE:])], axis=-1)
    wq = wq.reshape(MLA_Q_LORA, MLA_HEADS * 2 * LANES).astype(BF16)
    wkv = mla_w_kv_up.reshape(MLA_KV_LORA, MLA_HEADS, MLA_NOPE + MLA_V)
    wkv = jnp.concatenate([wkv[..., :MLA_NOPE].reshape(MLA_KV_LORA, -1),
                           wkv[..., MLA_NOPE:].reshape(MLA_KV_LORA, -1)], axis=1).astype(BF16)
    rope_tab = _rope_table(seq)

    nf = D_FF // FFN_COLS

    def ffn_cols(a):
        g = a[..., :D_FF].reshape(a.shape[:-1] + (nf, FFN_COLS))
        u = a[..., D_FF:].reshape(a.shape[:-1] + (nf, FFN_COLS))
        return jnp.moveaxis(jnp.concatenate([g, u], axis=-1), -2, 0)

    wup = ffn_cols(ffn_w_up).astype(BF16)
    cw = ffn_cols(ffn_conv_w)
    cb = ffn_cols(ffn_conv_b[None, :])
    wdn = ffn_w_down.reshape(nf, FFN_COLS, D_MODEL).astype(BF16)

    row = lambda v: v.reshape(1, -1)

    tm = PROJ_ROWS
    tok = lambda n: pl.BlockSpec((None, tm, n), lambda b, s: (b, s, 0))
    hd = lambda n: pl.BlockSpec((None, MLA_HEADS, tm, n), lambda b, s: (b, 0, s, 0))
    sd = jax.ShapeDtypeStruct
    gq, gk, gv, gz, gcol, mq, mk, mv = pl.pallas_call(
        _proj_kernel,
        grid=(bsz, seq // tm),
        in_specs=[tok(D_MODEL), _const_spec(w1.shape), _const_spec(gdn_conv_w.shape),
                  _const_spec(gpar.shape), _const_spec((1, MLA_Q_LORA)), _const_spec(wq.shape),
                  _const_spec((1, MLA_KV_LORA)), _const_spec(wkv.shape),
                  pl.BlockSpec((tm, LANES), lambda b, s: (s, 0))],
        out_specs=[tok(GDN_QK), tok(GDN_QK), tok(GDN_VW), tok(GDN_VW), tok(LANES),
                   hd(MLA_QKW), hd(MLA_QKW), hd(MLA_V)],
        out_shape=[sd((bsz, seq, GDN_QK), BF16), sd((bsz, seq, GDN_QK), BF16),
                   sd((bsz, seq, GDN_VW), BF16), sd((bsz, seq, GDN_VW), F32),
                   sd((bsz, seq, LANES), F32),
                   sd((bsz, MLA_HEADS, seq, MLA_QKW), BF16), sd((bsz, MLA_HEADS, seq, MLA_QKW), BF16),
                   sd((bsz, MLA_HEADS, seq, MLA_V), BF16)],
        scratch_shapes=[pltpu.VMEM((SUBLANES, 2 * GDN_QK + GDN_VW), F32)],
        compiler_params=_params("parallel", "arbitrary"),
        name="proj",
    )(x, w1, gdn_conv_w, gpar, row(mla_q_norm_g), wq, row(mla_kv_norm_g), wkv, rope_tab)

    cs = GDN_CHUNKS * CHUNK
    gtok = lambda n: pl.BlockSpec((None, cs, n), lambda b, s: (b, s, 0))
    out_a = pl.pallas_call(
        _gdn_kernel,
        grid=(bsz, seq // cs),
        in_specs=[gtok(GDN_QK), gtok(GDN_QK), gtok(GDN_VW), gtok(LANES), gtok(GDN_VW),
                  _const_spec((1, GDN_DV))],
        out_specs=gtok(GDN_VW),
        out_shape=sd((bsz, seq, GDN_VW), BF16),
        scratch_shapes=[pltpu.VMEM((GDN_HEADS, GDN_DK, GDN_DV), F32)],
        compiler_params=_params("parallel", "arbitrary"),
        name="gdn",
    )(gq, gk, gv, gcol, gz, row(gdn_norm_g))

    tq = ATTN_ROWS
    out_b = pl.pallas_call(
        _attn_kernel,
        grid=(bsz, MLA_HEADS, seq // tq),
        in_specs=[pl.BlockSpec((None, None, tq, MLA_QKW), lambda b, h, i: (b, h, i, 0)),
                  pl.BlockSpec((None, None, seq, MLA_QKW), lambda b, h, i: (b, h, 0, 0)),
                  pl.BlockSpec((None, None, seq, MLA_V), lambda b, h, i: (b, h, 0, 0))],
        out_specs=pl.BlockSpec((None, tq, MLA_V), lambda b, h, i: (b, i, h)),
        out_shape=sd((bsz, seq, MLA_HEADS * MLA_V), BF16),
        compiler_params=_params("parallel", "parallel", "arbitrary"),
        name="attn",
    )(mq, mk, mv)

    tf = FFN_ROWS
    ftok = lambda n: pl.BlockSpec((None, tf, n), lambda b, s: (b, s, 0))
    out = pl.pallas_call(
        _ffn_kernel,
        grid=(bsz, seq // tf),
        in_specs=[ftok(D_MODEL), ftok(GDN_VW), ftok(MLA_HEADS * MLA_V), ftok(PLE_DIM),
                  _const_spec((GDN_VW + MLA_HEADS * MLA_V, D_MODEL)),
                  _const_spec((1, D_MODEL)), _const_spec((1, D_MODEL)),
                  _const_spec(wup.shape), _const_spec(cw.shape), _const_spec(cb.shape),
                  _const_spec(wdn.shape), _const_spec((D_MODEL, D_MODEL)), _const_spec((1, D_MODEL)),
                  _const_spec((PLE_DIM, D_MODEL)), _const_spec((1, D_MODEL)), _const_spec((1, D_MODEL))],
        out_specs=ftok(D_MODEL),
        out_shape=sd((bsz, seq, D_MODEL), F32),
        scratch_shapes=[pltpu.VMEM((nf, SUBLANES, 2 * FFN_COLS), F32),
                        pltpu.VMEM((tf, D_MODEL), F32)],
        compiler_params=_params("parallel", "arbitrary"),
        name="ffn",
    )(x, out_a, out_b, p, w_out.astype(BF16), row(ln1_g), row(ln1_b), wup, cw, cb, wdn,
      ple_w_gate.astype(BF16), row(ple_b_gate), ple_w_proj.astype(BF16), row(ln2_g), row(ln2_b))
    return out


def kernel(x, p, w_in, gdn_conv_w, gdn_a_log, gdn_dt_bias, gdn_norm_g, mla_q_norm_g, mla_w_q_up,
           mla_kv_norm_g, mla_w_kv_up, w_out, ln1_g, ln1_b, ffn_w_up, ffn_conv_w, ffn_conv_b,
           ffn_w_down, ple_w_gate, ple_b_gate, ple_w_proj, ln2_g, ln2_b):
    h = x
    for i in range(DEPTH):
        h = _layer(h, p[i], w_in[i], gdn_conv_w[i], gdn_a_log[i], gdn_dt_bias[i], gdn_norm_g[i],
                   mla_q_norm_g[i], mla_w_q_up[i], mla_kv_norm_g[i], mla_w_kv_up[i], w_out[i],
                   ln1_g[i], ln1_b[i], ffn_w_up[i], ffn_conv_w[i], ffn_conv_b[i], ffn_w_down[i],
                   ple_w_gate[i], ple_b_gate[i], ple_w_proj[i], ln2_g[i], ln2_b[i])
    return h
```

```python
import functools
import math

import jax
import jax.numpy as jnp
from jax import lax
from jax.experimental import pallas as pl
from jax.experimental.pallas import tpu as pltpu

F32 = jnp.float32
BF16 = jnp.bfloat16

D_MODEL = 1024
CHUNK = 64
PLE_DIM = 256
GDN_HEADS = 4
GDN_DK = 128
GDN_DV = 128
GDN_CONV = 4
MLA_HEADS = 4
MLA_NOPE = 128
MLA_ROPE = 64
MLA_V = 128
MLA_Q_LORA = 384
MLA_KV_LORA = 256
ROPE_THETA = 10000.0
D_FF = 2816
FFN_CONV = 3
DEPTH = 1
ALPHA = (2.0 * DEPTH) ** 0.25
NORM_EPS = 1e-6
GDN_QK = GDN_HEADS * GDN_DK
GDN_VW = GDN_HEADS * GDN_DV
MLA_QKW = MLA_NOPE + MLA_ROPE
MLA_QSCALE = MLA_QKW ** -0.5 * math.log2(math.e)

LANES = 128
SUBLANES = 8
VMEM_LIMIT_BYTES = 56 * 1024 * 1024

PROJ_ROWS = 512
GDN_CHUNKS = 8
GDN_SEQS = 2
ATTN_ROWS = 512
FFN_ROWS = 512
FFN_COLS = 256

C_QKV = 0
C_Z = C_QKV + 2 * GDN_QK + GDN_VW
C_CQ = C_Z + GDN_VW
C_CKV = C_CQ + MLA_Q_LORA
C_KR = C_CKV + MLA_KV_LORA
C_AB = C_KR + 2 * MLA_ROPE
C_END = C_AB + LANES


def _dot(a, b):
    return jnp.dot(a, b, preferred_element_type=F32)


def _dot_nt(a, b):
    return lax.dot_general(a, b, (((1,), (1,)), ((), ())), preferred_element_type=F32)


def _dot_tn(a, b):
    return lax.dot_general(a, b, (((0,), (0,)), ((), ())), preferred_element_type=F32)


def _sigmoid(x):
    return 1.0 / (1.0 + jnp.exp(-x))


def _silu(x):
    return x * _sigmoid(x)


def _softplus(x):
    return jnp.maximum(x, 0.0) + jnp.log1p(jnp.exp(-jnp.abs(x)))


def _layernorm(x, g, b):
    mu = jnp.mean(x, axis=-1, keepdims=True)
    xc = x - mu
    var = jnp.mean(xc * xc, axis=-1, keepdims=True)
    return xc * lax.rsqrt(var + NORM_EPS) * g + b


def _rmsnorm(x, g):
    return x * lax.rsqrt(jnp.mean(x * x, axis=-1, keepdims=True) + NORM_EPS) * g


def _causal_conv(ext, w_ref, cols, taps):
    y = ext * w_ref[taps - 1:taps, cols]
    for d in range(1, taps):
        y = y + pltpu.roll(ext, d, 0) * w_ref[taps - 1 - d:taps - d, cols]
    return y[SUBLANES:]


def _interleave_rows(a, perm_ref):
    r8 = a.shape[0] // SUBLANES
    for j in range(a.shape[1] // LANES):
        for k in range(SUBLANES):
            perm_ref[j, pl.ds(k, r8, stride=SUBLANES), :] = a[k * r8:(k + 1) * r8, j * LANES:(j + 1) * LANES]


def _deinterleave_rows(a, perm_ref):
    r8 = a.shape[0] // SUBLANES
    cols = []
    for j in range(a.shape[1] // LANES):
        perm_ref[j] = a[:, j * LANES:(j + 1) * LANES]
        cols.append(jnp.concatenate(
            [perm_ref[j, pl.ds(k, r8, stride=SUBLANES), :] for k in range(SUBLANES)], axis=0))
    return jnp.concatenate(cols, axis=-1)


def _roll_groups(a):
    return jnp.concatenate([pltpu.roll(a[g:g + SUBLANES], 1, 0) for g in range(0, a.shape[0], SUBLANES)],
                           axis=0)


def _proj_kernel(x_ref, w1_ref, convw_ref, gpar_ref, qng_ref, wq_ref, kvng_ref, wkv_ref, rope_ref,
                 gq_ref, gk_ref, gv_ref, gz_ref, gcol_ref, grow_ref, mq_ref, mk_ref, mv_ref,
                 carry_ref, perm_ref, unperm_ref):
    rows = x_ref.shape[0]
    halo = (GDN_CONV - 1) * SUBLANES
    wide = 2 * LANES

    @pl.when(pl.program_id(1) == 0)
    def _():
        carry_ref[...] = jnp.zeros_like(carry_ref)

    x = x_ref[...]
    xb = x.astype(BF16)

    _interleave_rows(x, perm_ref)
    xp = jnp.concatenate([perm_ref[j] for j in range(D_MODEL // LANES)], axis=-1).astype(BF16)
    first = lax.broadcasted_iota(jnp.int32, (halo, wide), 0) % SUBLANES == 0
    outs = (gq_ref, gk_ref, gv_ref)
    for part in range(3):
        for pair in range(GDN_QK // wide):
            cols = slice(part * GDN_QK + pair * wide, part * GDN_QK + (pair + 1) * wide)
            pre = _dot(xp, w1_ref[:, cols])
            tail = pre[rows - halo:]
            head = jnp.where(first, _roll_groups(carry_ref[:, cols]), _roll_groups(tail))
            carry_ref[:, cols] = tail
            ext = jnp.concatenate([head, pre], axis=0)
            y = ext[halo:] * convw_ref[GDN_CONV - 1:GDN_CONV, cols]
            for d in range(1, GDN_CONV):
                y = y + ext[halo - d * SUBLANES:halo - d * SUBLANES + rows] * convw_ref[GDN_CONV - 1 - d:GDN_CONV - d, cols]
            y = _silu(y)
            for j in range(wide // GDN_DK):
                yh = y[:, j * GDN_DK:(j + 1) * GDN_DK]
                if part < 2:
                    yh = yh * lax.rsqrt(jnp.sum(yh * yh, axis=-1, keepdims=True) + NORM_EPS)
                if part == 0:
                    yh = yh * (GDN_DK ** -0.5)
                h = pair * (wide // GDN_DK) + j
                outs[part][:, h * GDN_DK:(h + 1) * GDN_DK] = _deinterleave_rows(yh, unperm_ref).astype(BF16)

    gz_ref[...] = _dot(xb, w1_ref[:, C_Z:C_Z + GDN_VW])

    kr_ab = _dot(xb, w1_ref[:, C_KR:C_KR + wide])
    ab = kr_ab[:, LANES:]
    lane = lax.broadcasted_iota(jnp.int32, ab.shape, 1)
    rpos = lax.broadcasted_iota(jnp.int32, ab.shape, 0) & (CHUNK - 1)
    cs = -jnp.exp(gpar_ref[0:1, :]) * _softplus(ab + gpar_ref[1:2, :])
    sh = 1
    while sh < CHUNK:
        cs = cs + jnp.where(rpos >= sh, pltpu.roll(cs, sh, 0), 0.0)
        sh *= 2
    gcol = jnp.where(lane < GDN_HEADS, cs, _sigmoid(ab))
    gcol_ref[...] = gcol
    grow_ref[...] = gcol.T[:SUBLANES]

    tab = rope_ref[...]
    half = lax.broadcasted_iota(jnp.int32, tab.shape, 1) < MLA_ROPE

    def rope(r):
        t = r * tab
        return jnp.where(half, t + pltpu.roll(t, MLA_ROPE, 1), 0.0)

    krope = rope(kr_ab[:, :LANES])[:, :MLA_ROPE].astype(BF16)
    cq = _rmsnorm(_dot(xb, w1_ref[:, C_CQ:C_CQ + MLA_Q_LORA]), qng_ref[...]).astype(BF16)
    ckv = _rmsnorm(_dot(xb, w1_ref[:, C_CKV:C_CKV + MLA_KV_LORA]), kvng_ref[...]).astype(BF16)
    for h in range(MLA_HEADS):
        qh = _dot(cq, wq_ref[:, h * wide:(h + 1) * wide])
        qr = rope(qh[:, LANES:])[:, :MLA_ROPE]
        mq_ref[h] = (jnp.concatenate([qh[:, :LANES], qr], axis=-1) * MLA_QSCALE).astype(BF16)
    for pair in range(MLA_HEADS * MLA_NOPE // wide):
        kn = _dot(ckv, wkv_ref[:, pair * wide:(pair + 1) * wide]).astype(BF16)
        vv = _dot(ckv, wkv_ref[:, MLA_HEADS * MLA_NOPE + pair * wide:MLA_HEADS * MLA_NOPE + (pair + 1) * wide])
        for j in range(wide // MLA_NOPE):
            h = pair * (wide // MLA_NOPE) + j
            mk_ref[h] = jnp.concatenate([kn[:, j * MLA_NOPE:(j + 1) * MLA_NOPE], krope], axis=-1)
            mv_ref[h] = vv[:, j * MLA_V:(j + 1) * MLA_V].astype(BF16)


def _gdn_kernel(q_ref, k_ref, v_ref, gcol_ref, grow_ref, z_ref, ng_ref, o_ref, state_ref):
    @pl.when(pl.program_id(1) == 0)
    def _():
        state_ref[...] = jnp.zeros_like(state_ref)

    c = CHUNK
    nh = GDN_HEADS
    nc = GDN_CHUNKS
    nb = q_ref.shape[0]
    per_chunk = nb * nh
    ng_all = nc * per_chunk
    ri = lax.broadcasted_iota(jnp.int32, (ng_all, c, c), 1)
    ci = lax.broadcasted_iota(jnp.int32, (ng_all, c, c), 2)
    incl = ri >= ci
    strict = ri > ci
    eye = jnp.where(ri == ci, 1.0, 0.0).astype(F32)
    bmm = functools.partial(jnp.einsum, preferred_element_type=F32)

    def problems(fn):
        return jnp.stack([fn(b, slice(n * c, (n + 1) * c), h)
                          for n in range(nc) for b in range(nb) for h in range(nh)])

    def heads(ref):
        return problems(lambda b, rs, h: ref[b, rs, h * GDN_DK:(h + 1) * GDN_DK])

    gc_c = problems(lambda b, rs, h: gcol_ref[b, rs, h:h + 1])
    beta = problems(lambda b, rs, h: gcol_ref[b, rs, nh + h:nh + h + 1])
    gc_r = problems(lambda b, rs, h: grow_ref[b, h:h + 1, rs])
    g_last = gc_r[:, :, c - 1:c]
    decay = jnp.where(incl, jnp.exp(gc_c - gc_r), 0.0)
    eg = jnp.exp(gc_c)
    q = heads(q_ref)
    k = heads(k_ref)
    kf = k.astype(F32)
    kb = kf * beta
    kk = bmm("gid,gjd->gij", jnp.concatenate([kb.astype(BF16), q], axis=1), k)
    neg_a = jnp.where(strict, -(kk[:, :c] * decay), 0.0)
    qk = (kk[:, c:] * decay).astype(BF16)
    mp = jnp.concatenate([neg_a, eye], axis=-1)
    right = lax.broadcasted_iota(jnp.int32, mp.shape, 2) >= c
    level = 1
    while level < c:
        mp = bmm("gij,gjk->gik", mp[:, :, :c].astype(BF16), mp.astype(BF16)) + jnp.where(right, mp, 0.0)
        level *= 2
    inv = mp[:, :, c:]
    rhs = jnp.concatenate([kb * eg, heads(v_ref).astype(F32) * beta], axis=-1)
    wu = bmm("gij,gjd->gid", inv.astype(BF16), rhs.astype(BF16))
    kd_t = jnp.swapaxes(kf * jnp.exp(g_last - gc_c), 1, 2).astype(BF16)
    lhs2 = jnp.concatenate([qk, kd_t], axis=1)
    lhs1 = jnp.concatenate([wu[:, :, :GDN_DK], q.astype(F32) * eg], axis=1).astype(BF16)
    u = wu[:, :, GDN_DK:]
    g_decay = jnp.exp(g_last)

    state = state_ref[...]
    outs = []
    for n in range(nc):
        gs = slice(n * per_chunk, (n + 1) * per_chunk)
        ws_qs = bmm("hid,hdv->hiv", lhs1[gs], state.astype(BF16))
        v_new = (u[gs] - ws_qs[:, :c]).astype(BF16)
        upd = bmm("hij,hjv->hiv", lhs2[gs], v_new)
        state = state * g_decay[gs] + upd[:, c:]
        outs.append(ws_qs[:, c:] + upd[:, :c])
    state_ref[...] = state

    for b in range(nb):
        for h in range(nh):
            hs = slice(h * GDN_DV, (h + 1) * GDN_DV)
            oh = jnp.concatenate([o[b * nh + h] for o in outs], axis=0)
            oh = oh * lax.rsqrt(jnp.mean(oh * oh, axis=-1, keepdims=True) + NORM_EPS) * ng_ref[...]
            o_ref[b, :, hs] = (oh * _silu(z_ref[b, :, hs])).astype(BF16)


def _attn_kernel(q_ref, k_ref, v_ref, o_ref):
    t = ATTN_ROWS
    nh = MLA_HEADS
    qi = pl.program_id(1)
    q = q_ref[...]
    bmm = functools.partial(jnp.einsum, preferred_element_type=F32)

    ones = jnp.ones((nh, t, LANES), BF16)

    def block(kb, carry, masked):
        m, acc = carry
        start = pl.multiple_of(kb * t, t)
        s = bmm("hqd,hkd->hqk", q, k_ref[:, pl.ds(start, t), :])
        if masked:
            qc = lax.broadcasted_iota(jnp.int32, s.shape, 1) // CHUNK
            kc = lax.broadcasted_iota(jnp.int32, s.shape, 2) // CHUNK
            s = jnp.where(kc <= qc, s, -jnp.inf)
        m_new = jnp.maximum(m, jnp.max(s, axis=-1, keepdims=True))
        p = jnp.exp2(s - m_new).astype(BF16)
        v1 = jnp.concatenate([v_ref[:, pl.ds(start, t), :], ones], axis=-1)
        acc = jnp.exp2(m - m_new) * acc + bmm("hqk,hkd->hqd", p, v1)
        return m_new, acc

    init = (jnp.full((nh, t, 1), -jnp.inf, F32), jnp.zeros((nh, t, MLA_V + LANES), F32))
    carry = lax.fori_loop(0, qi, lambda kb, cr: block(kb, cr, False), init)
    _, acc = block(qi, carry, True)
    for h in range(nh):
        o_ref[:, h * MLA_V:(h + 1) * MLA_V] = (acc[h, :, :MLA_V] / acc[h, :, MLA_V:MLA_V + 1]).astype(BF16)


def _ffn_kernel(x_ref, oa_ref, ob_ref, p_ref, wo_ref, ln1g_ref, ln1b_ref, wup_ref, cw_ref, cb_ref,
                wdn_ref, wg_ref, bg_ref, wp_ref, ln2g_ref, ln2b_ref, out_ref, carry_ref, perm_ref, act_ref):
    rows = x_ref.shape[0]
    halo = (FFN_CONV - 1) * SUBLANES

    @pl.when(pl.program_id(1) == 0)
    def _():
        carry_ref[...] = jnp.zeros_like(carry_ref)

    mix = _dot(oa_ref[...], wo_ref[:GDN_VW, :]) + _dot(ob_ref[...], wo_ref[GDN_VW:, :])
    h = _layernorm(ALPHA * x_ref[...] + mix, ln1g_ref[...], ln1b_ref[...])
    hb = h.astype(BF16)
    ple = _sigmoid(_dot(hb, wg_ref[...]) + bg_ref[...]) * _dot(p_ref[...].astype(BF16), wp_ref[...])
    resid = ALPHA * h + ple

    _interleave_rows(h, perm_ref)
    hp = jnp.concatenate([perm_ref[j] for j in range(D_MODEL // LANES)], axis=-1).astype(BF16)
    first = lax.broadcasted_iota(jnp.int32, (halo, FFN_COLS), 0) % SUBLANES == 0

    def up_conv(cols):
        u = _dot(hp, wup_ref[:, cols])
        tail = u[rows - halo:]
        head = jnp.where(first, _roll_groups(carry_ref[:, cols]), _roll_groups(tail))
        carry_ref[:, cols] = tail
        ext = jnp.concatenate([head, u], axis=0)
        y = cb_ref[:, cols]
        for d in range(FFN_CONV):
            y = y + ext[halo - d * SUBLANES:halo - d * SUBLANES + rows] * cw_ref[FFN_CONV - 1 - d:FFN_CONV - d, cols]
        return y

    for f in range(D_FF // FFN_COLS):
        gate = up_conv(slice(f * FFN_COLS, (f + 1) * FFN_COLS))
        up = up_conv(slice(D_FF + f * FFN_COLS, D_FF + (f + 1) * FFN_COLS))
        act_ref[:, f * FFN_COLS:(f + 1) * FFN_COLS] = (_silu(gate) * up).astype(BF16)
    ffn = _deinterleave_rows(_dot(act_ref[...], wdn_ref[...]), perm_ref)
    out_ref[...] = _layernorm(resid + ffn, ln2g_ref[...], ln2b_ref[...])


def _const_spec(shape):
    nd = len(shape)
    return pl.BlockSpec(shape, lambda *_: (0,) * nd, pipeline_mode=pl.Buffered(1))


def _params(*sem):
    return pltpu.CompilerParams(dimension_semantics=sem, vmem_limit_bytes=VMEM_LIMIT_BYTES)


def _swap_halves(w):
    half = w.shape[-1] // 2
    return jnp.concatenate([w[..., half:], w[..., :half]], axis=-1)


def _rope_table(seq):
    inv = ROPE_THETA ** (-jnp.arange(0, MLA_ROPE, 2, dtype=F32) / MLA_ROPE)
    ang = jnp.arange(seq, dtype=F32)[:, None] * inv[None, :]
    cos, sin = jnp.cos(ang), jnp.sin(ang)
    return jnp.concatenate([cos, cos, -sin, sin], axis=-1)


def _layer(x, p, w_in, gdn_conv_w, gdn_a_log, gdn_dt_bias, gdn_norm_g, mla_q_norm_g, mla_w_q_up,
           mla_kv_norm_g, mla_w_kv_up, w_out, ln1_g, ln1_b, ffn_w_up, ffn_conv_w, ffn_conv_b,
           ffn_w_down, ple_w_gate, ple_b_gate, ple_w_proj, ln2_g, ln2_b):
    bsz, seq, _ = x.shape
    assert seq % PROJ_ROWS == 0 and seq % (GDN_CHUNKS * CHUNK) == 0
    assert seq % ATTN_ROWS == 0 and seq % FFN_ROWS == 0 and D_FF % FFN_COLS == 0

    o_z = 2 * GDN_QK + GDN_VW
    o_a = o_z + GDN_VW
    o_cq = o_a + 2 * GDN_HEADS
    o_ckv = o_cq + MLA_Q_LORA
    o_kr = o_ckv + MLA_KV_LORA
    w_kr = w_in[:, o_kr:o_kr + MLA_ROPE]
    w1 = jnp.concatenate([
        w_in[:, :o_a], w_in[:, o_cq:o_kr], w_kr, _swap_halves(w_kr), w_in[:, o_a:o_cq],
        jnp.zeros((D_MODEL, LANES - 2 * GDN_HEADS), F32)], axis=1).astype(BF16)
    assert w1.shape[1] == C_END
    gpar = jnp.zeros((2, LANES), F32)
    gpar = gpar.at[0, :GDN_HEADS].set(gdn_a_log).at[1, :GDN_HEADS].set(gdn_dt_bias)
    wq = mla_w_q_up.reshape(MLA_Q_LORA, MLA_HEADS, MLA_QKW)
    wq = jnp.concatenate([wq, _swap_halves(wq[..., MLA_NOPE:])], axis=-1)
    wq = wq.reshape(MLA_Q_LORA, MLA_HEADS * 2 * LANES).astype(BF16)
    wkv = mla_w_kv_up.reshape(MLA_KV_LORA, MLA_HEADS, MLA_NOPE + MLA_V)
    wkv = jnp.concatenate([wkv[..., :MLA_NOPE].reshape(MLA_KV_LORA, -1),
                           wkv[..., MLA_NOPE:].reshape(MLA_KV_LORA, -1)], axis=1).astype(BF16)
    rope_tab = _rope_table(seq)

    wup = ffn_w_up.astype(BF16)
    wdn = ffn_w_down.astype(BF16)

    row = lambda v: v.reshape(1, -1)

    tm = PROJ_ROWS
    tok = lambda n: pl.BlockSpec((None, tm, n), lambda b, s: (b, s, 0))
    hd = lambda n: pl.BlockSpec((None, MLA_HEADS, tm, n), lambda b, s: (b, 0, s, 0))
    sd = jax.ShapeDtypeStruct
    gq, gk, gv, gz, gcol, grow, mq, mk, mv = pl.pallas_call(
        _proj_kernel,
        grid=(bsz, seq // tm),
        in_specs=[tok(D_MODEL), _const_spec(w1.shape), _const_spec(gdn_conv_w.shape),
                  _const_spec(gpar.shape), _const_spec((1, MLA_Q_LORA)), _const_spec(wq.shape),
                  _const_spec((1, MLA_KV_LORA)), _const_spec(wkv.shape),
                  pl.BlockSpec((tm, LANES), lambda b, s: (s, 0))],
        out_specs=[tok(GDN_QK), tok(GDN_QK), tok(GDN_VW), tok(GDN_VW), tok(LANES),
                   pl.BlockSpec((None, SUBLANES, tm), lambda b, s: (b, 0, s)),
                   hd(MLA_QKW), hd(MLA_QKW), hd(MLA_V)],
        out_shape=[sd((bsz, seq, GDN_QK), BF16), sd((bsz, seq, GDN_QK), BF16),
                   sd((bsz, seq, GDN_VW), BF16), sd((bsz, seq, GDN_VW), F32),
                   sd((bsz, seq, LANES), F32), sd((bsz, SUBLANES, seq), F32),
                   sd((bsz, MLA_HEADS, seq, MLA_QKW), BF16), sd((bsz, MLA_HEADS, seq, MLA_QKW), BF16),
                   sd((bsz, MLA_HEADS, seq, MLA_V), BF16)],
        scratch_shapes=[pltpu.VMEM(((GDN_CONV - 1) * SUBLANES, 2 * GDN_QK + GDN_VW), F32),
                        pltpu.VMEM((D_MODEL // LANES, tm, LANES), F32),
                        pltpu.VMEM((1, tm, LANES), F32)],
        compiler_params=_params("parallel", "arbitrary"),
        name="proj",
    )(x, w1, gdn_conv_w, gpar, row(mla_q_norm_g), wq, row(mla_kv_norm_g), wkv, rope_tab)

    cs = GDN_CHUNKS * CHUNK
    gb = GDN_SEQS if bsz % GDN_SEQS == 0 else 1
    gtok = lambda n: pl.BlockSpec((gb, cs, n), lambda b, s: (b, s, 0))
    out_a = pl.pallas_call(
        _gdn_kernel,
        grid=(bsz // gb, seq // cs),
        in_specs=[gtok(GDN_QK), gtok(GDN_QK), gtok(GDN_VW), gtok(LANES),
                  pl.BlockSpec((gb, SUBLANES, cs), lambda b, s: (b, 0, s)), gtok(GDN_VW),
                  _const_spec((1, GDN_DV))],
        out_specs=gtok(GDN_VW),
        out_shape=sd((bsz, seq, GDN_VW), BF16),
        scratch_shapes=[pltpu.VMEM((gb * GDN_HEADS, GDN_DK, GDN_DV), F32)],
        compiler_params=_params("parallel", "arbitrary"),
        name="gdn",
    )(gq, gk, gv, gcol, grow, gz, row(gdn_norm_g))

    tq = ATTN_ROWS
    out_b = pl.pallas_call(
        _attn_kernel,
        grid=(bsz, seq // tq),
        in_specs=[pl.BlockSpec((None, MLA_HEADS, tq, MLA_QKW), lambda b, i: (b, 0, i, 0)),
                  pl.BlockSpec((None, MLA_HEADS, seq, MLA_QKW), lambda b, i: (b, 0, 0, 0)),
                  pl.BlockSpec((None, MLA_HEADS, seq, MLA_V), lambda b, i: (b, 0, 0, 0))],
        out_specs=pl.BlockSpec((None, tq, MLA_HEADS * MLA_V), lambda b, i: (b, i, 0)),
        out_shape=sd((bsz, seq, MLA_HEADS * MLA_V), BF16),
        compiler_params=_params("parallel", "arbitrary"),
        name="attn",
    )(mq, mk, mv)

    tf = FFN_ROWS
    ftok = lambda n: pl.BlockSpec((None, tf, n), lambda b, s: (b, s, 0))
    out = pl.pallas_call(
        _ffn_kernel,
        grid=(bsz, seq // tf),
        in_specs=[ftok(D_MODEL), ftok(GDN_VW), ftok(MLA_HEADS * MLA_V), ftok(PLE_DIM),
                  _const_spec((GDN_VW + MLA_HEADS * MLA_V, D_MODEL)),
                  _const_spec((1, D_MODEL)), _const_spec((1, D_MODEL)),
                  _const_spec(wup.shape), _const_spec(ffn_conv_w.shape), _const_spec((1, 2 * D_FF)),
                  _const_spec(wdn.shape), _const_spec((D_MODEL, D_MODEL)), _const_spec((1, D_MODEL)),
                  _const_spec((PLE_DIM, D_MODEL)), _const_spec((1, D_MODEL)), _const_spec((1, D_MODEL))],
        out_specs=ftok(D_MODEL),
        out_shape=sd((bsz, seq, D_MODEL), F32),
        scratch_shapes=[pltpu.VMEM(((FFN_CONV - 1) * SUBLANES, 2 * D_FF), F32),
                        pltpu.VMEM((D_MODEL // LANES, tf, LANES), F32),
                        pltpu.VMEM((tf, D_FF), BF16)],
        compiler_params=_params("parallel", "arbitrary"),
        name="ffn",
    )(x, out_a, out_b, p, w_out.astype(BF16), row(ln1_g), row(ln1_b), wup, ffn_conv_w, row(ffn_conv_b), wdn,
      ple_w_gate.astype(BF16), row(ple_b_gate), ple_w_proj.astype(BF16), row(ln2_g), row(ln2_b))
    return out


def kernel(x, p, w_in, gdn_conv_w, gdn_a_log, gdn_dt_bias, gdn_norm_g, mla_q_norm_g, mla_w_q_up,
           mla_kv_norm_g, mla_w_kv_up, w_out, ln1_g, ln1_b, ffn_w_up, ffn_conv_w, ffn_conv_b,
           ffn_w_down, ple_w_gate, ple_b_gate, ple_w_proj, ln2_g, ln2_b):
    h = x
    for i in range(DEPTH):
        h = _layer(h, p[i], w_in[i], gdn_conv_w[i], gdn_a_log[i], gdn_dt_bias[i], gdn_norm_g[i],
                   mla_q_norm_g[i], mla_w_q_up[i], mla_kv_norm_g[i], mla_w_kv_up[i], w_out[i],
                   ln1_g[i], ln1_b[i], ffn_w_up[i], ffn_conv_w[i], ffn_conv_b[i], ffn_w_down[i],
                   ple_w_gate[i], ple_b_gate[i], ple_w_proj[i], ln2_g[i], ln2_b[i])
    return h
```

```python
import functools
import math

import jax
import jax.numpy as jnp
from jax import lax
from jax.experimental import pallas as pl
from jax.experimental.pallas import tpu as pltpu

F32 = jnp.float32
BF16 = jnp.bfloat16

D_MODEL = 1024
CHUNK = 64
PLE_DIM = 256
GDN_HEADS = 4
GDN_DK = 128
GDN_DV = 128
GDN_CONV = 4
MLA_HEADS = 4
MLA_NOPE = 128
MLA_ROPE = 64
MLA_V = 128
MLA_Q_LORA = 384
MLA_KV_LORA = 256
ROPE_THETA = 10000.0
D_FF = 2816
FFN_CONV = 3
DEPTH = 1
ALPHA = (2.0 * DEPTH) ** 0.25
NORM_EPS = 1e-6
GDN_QK = GDN_HEADS * GDN_DK
GDN_VW = GDN_HEADS * GDN_DV
MLA_QKW = MLA_NOPE + MLA_ROPE
MLA_QSCALE = MLA_QKW ** -0.5 * math.log2(math.e)

LANES = 128
SUBLANES = 8
VMEM_LIMIT_BYTES = 56 * 1024 * 1024

PROJ_ROWS = 512
PROJ_STREAMS = 1
GDN_CHUNKS = 8
GDN_SEQS = 2
ATTN_ROWS = 512
ATTN_ONES_ROWS = 16
FFN_ROWS = 512
FFN_COLS = 256
FFN_STREAMS = 2

C_QKV = 0
C_Z = C_QKV + 2 * GDN_QK + GDN_VW
C_CQ = C_Z + GDN_VW
C_CKV = C_CQ + MLA_Q_LORA
C_KR = C_CKV + MLA_KV_LORA
C_AB = C_KR + 2 * MLA_ROPE
C_END = C_AB + LANES


def _dot(a, b):
    return jnp.dot(a, b, preferred_element_type=F32)


def _dot_nt(a, b):
    return lax.dot_general(a, b, (((1,), (1,)), ((), ())), preferred_element_type=F32)


def _dot_tn(a, b):
    return lax.dot_general(a, b, (((0,), (0,)), ((), ())), preferred_element_type=F32)


def _sigmoid(x):
    return 1.0 / (1.0 + jnp.exp(-x))


def _silu(x):
    return x * _sigmoid(x)


def _softplus(x):
    return jnp.maximum(x, 0.0) + jnp.log1p(jnp.exp(-jnp.abs(x)))


def _layernorm(x, g, b):
    mu = jnp.mean(x, axis=-1, keepdims=True)
    xc = x - mu
    var = jnp.mean(xc * xc, axis=-1, keepdims=True)
    return xc * lax.rsqrt(var + NORM_EPS) * g + b


def _rmsnorm(x, g):
    return x * lax.rsqrt(jnp.mean(x * x, axis=-1, keepdims=True) + NORM_EPS) * g


def _causal_conv(ext, w_ref, cols, taps):
    y = ext * w_ref[taps - 1:taps, cols]
    for d in range(1, taps):
        y = y + pltpu.roll(ext, d, 0) * w_ref[taps - 1 - d:taps - d, cols]
    return y[SUBLANES:]


def _interleave_rows(a, perm_ref):
    r8 = a.shape[0] // SUBLANES
    for j in range(a.shape[1] // LANES):
        for k in range(SUBLANES):
            perm_ref[j, pl.ds(k, r8, stride=SUBLANES), :] = a[k * r8:(k + 1) * r8, j * LANES:(j + 1) * LANES]


def _deinterleave_rows(a, perm_ref):
    r8 = a.shape[0] // SUBLANES
    cols = []
    for j in range(a.shape[1] // LANES):
        perm_ref[j] = a[:, j * LANES:(j + 1) * LANES]
        cols.append(jnp.concatenate(
            [perm_ref[j, pl.ds(k, r8, stride=SUBLANES), :] for k in range(SUBLANES)], axis=0))
    return jnp.concatenate(cols, axis=-1)


def _roll_groups(a):
    return jnp.concatenate([pltpu.roll(a[g:g + SUBLANES], 1, 0) for g in range(0, a.shape[0], SUBLANES)],
                           axis=0)


def _proj_kernel(x_ref, w1_ref, convw_ref, gpar_ref, qng_ref, wq_ref, kvng_ref, wkv_ref, wvt_ref, rope_ref,
                 gq_ref, gk_ref, gv_ref, gz_ref, gcol_ref, grow_ref, mq_ref, mk_ref, mv_ref,
                 carry_ref, perm_ref, unperm_ref):
    ns = perm_ref.shape[0]
    rows = x_ref.shape[0] // ns
    streams = [slice(i * rows, (i + 1) * rows) for i in range(ns)]
    halo = (GDN_CONV - 1) * SUBLANES
    wide = 2 * LANES

    @pl.when(pl.program_id(1) == 0)
    def _():
        carry_ref[...] = jnp.zeros_like(carry_ref)

    x = [x_ref[s, :] for s in streams]
    xb = [xi.astype(BF16) for xi in x]

    xp = []
    for i in range(ns):
        _interleave_rows(x[i], perm_ref.at[i])
        xp.append(jnp.concatenate([perm_ref[i, j] for j in range(D_MODEL // LANES)], axis=-1).astype(BF16))
    first = lax.broadcasted_iota(jnp.int32, (halo, wide), 0) % SUBLANES == 0
    outs = (gq_ref, gk_ref, gv_ref)
    for part in range(3):
        for pair in range(GDN_QK // wide):
            cols = slice(part * GDN_QK + pair * wide, part * GDN_QK + (pair + 1) * wide)
            prev_tail = carry_ref[:, cols]
            for i in range(ns):
                pre = _dot(xp[i], w1_ref[:, cols])
                tail = pre[rows - halo:]
                head = jnp.where(first, _roll_groups(prev_tail), _roll_groups(tail))
                prev_tail = tail
                ext = jnp.concatenate([head, pre], axis=0)
                y = ext[halo:] * convw_ref[GDN_CONV - 1:GDN_CONV, cols]
                for d in range(1, GDN_CONV):
                    y = y + ext[halo - d * SUBLANES:halo - d * SUBLANES + rows] * convw_ref[GDN_CONV - 1 - d:GDN_CONV - d, cols]
                y = _silu(y)
                for j in range(wide // GDN_DK):
                    yh = y[:, j * GDN_DK:(j + 1) * GDN_DK]
                    if part < 2:
                        yh = yh * lax.rsqrt(jnp.sum(yh * yh, axis=-1, keepdims=True) + NORM_EPS)
                    if part == 0:
                        yh = yh * (GDN_DK ** -0.5)
                    h = pair * (wide // GDN_DK) + j
                    outs[part][streams[i], h * GDN_DK:(h + 1) * GDN_DK] = _deinterleave_rows(
                        yh, unperm_ref.at[i]).astype(BF16)
            carry_ref[:, cols] = prev_tail

    lane = lax.broadcasted_iota(jnp.int32, (rows, LANES), 1)
    rpos = lax.broadcasted_iota(jnp.int32, (rows, LANES), 0) & (CHUNK - 1)
    half = lane < MLA_ROPE
    for i in range(ns):
        s = streams[i]
        gz_ref[s, :] = _dot(xb[i], w1_ref[:, C_Z:C_Z + GDN_VW])

        kr_ab = _dot(xb[i], w1_ref[:, C_KR:C_KR + wide])
        ab = kr_ab[:, LANES:]
        cs = -jnp.exp(gpar_ref[0:1, :]) * _softplus(ab + gpar_ref[1:2, :])
        sh = 1
        while sh < CHUNK:
            cs = cs + jnp.where(rpos >= sh, pltpu.roll(cs, sh, 0), 0.0)
            sh *= 2
        gcol = jnp.where(lane < GDN_HEADS, cs, _sigmoid(ab))
        gcol_ref[s, :] = gcol
        grow_ref[:, s] = gcol.T[:SUBLANES]

        tab = rope_ref[s, :]

        def rope(r, tab=tab):
            t = r * tab
            return jnp.where(half, t + pltpu.roll(t, MLA_ROPE, 1), 0.0)

        krope = rope(kr_ab[:, :LANES])[:, :MLA_ROPE].astype(BF16)
        cq = _rmsnorm(_dot(xb[i], w1_ref[:, C_CQ:C_CQ + MLA_Q_LORA]), qng_ref[...]).astype(BF16)
        ckv = _rmsnorm(_dot(xb[i], w1_ref[:, C_CKV:C_CKV + MLA_KV_LORA]), kvng_ref[...]).astype(BF16)
        for h in range(MLA_HEADS):
            qh = _dot(cq, wq_ref[:, h * wide:(h + 1) * wide])
            qr = rope(qh[:, LANES:])[:, :MLA_ROPE]
            mq_ref[h, s, :] = (jnp.concatenate([qh[:, :LANES], qr], axis=-1) * MLA_QSCALE).astype(BF16)
        for pair in range(MLA_HEADS * MLA_NOPE // wide):
            kn = _dot(ckv, wkv_ref[:, pair * wide:(pair + 1) * wide]).astype(BF16)
            v_t = _dot_nt(wvt_ref[pair * wide:(pair + 1) * wide, :], ckv).astype(BF16)
            for j in range(wide // MLA_NOPE):
                h = pair * (wide // MLA_NOPE) + j
                mk_ref[h, s, :] = jnp.concatenate([kn[:, j * MLA_NOPE:(j + 1) * MLA_NOPE], krope], axis=-1)
                mv_ref[h, :, s] = v_t[j * MLA_V:(j + 1) * MLA_V, :]


def _gdn_kernel(q_ref, k_ref, v_ref, gcol_ref, grow_ref, z_ref, ng_ref, o_ref, state_ref):
    @pl.when(pl.program_id(1) == 0)
    def _():
        state_ref[...] = jnp.zeros_like(state_ref)

    c = CHUNK
    nh = GDN_HEADS
    nc = GDN_CHUNKS
    nb = q_ref.shape[0]
    per_chunk = nb * nh
    ng_all = nc * per_chunk
    ri = lax.broadcasted_iota(jnp.int32, (ng_all, c, c), 1)
    ci = lax.broadcasted_iota(jnp.int32, (ng_all, c, c), 2)
    incl = ri >= ci
    strict = ri > ci
    eye = jnp.where(ri == ci, 1.0, 0.0).astype(F32)
    bmm = functools.partial(jnp.einsum, preferred_element_type=F32)

    def problems(fn):
        return jnp.stack([fn(b, slice(n * c, (n + 1) * c), h)
                          for n in range(nc) for b in range(nb) for h in range(nh)])

    def heads(ref):
        return problems(lambda b, rs, h: ref[b, rs, h * GDN_DK:(h + 1) * GDN_DK])

    gc_c = problems(lambda b, rs, h: gcol_ref[b, rs, h:h + 1])
    beta = problems(lambda b, rs, h: gcol_ref[b, rs, nh + h:nh + h + 1])
    gc_r = problems(lambda b, rs, h: grow_ref[b, h:h + 1, rs])
    g_last = gc_r[:, :, c - 1:c]
    decay = jnp.where(incl, jnp.exp(gc_c - gc_r), 0.0)
    eg = jnp.exp(gc_c)
    q = heads(q_ref)
    k = heads(k_ref)
    kf = k.astype(F32)
    kb = kf * beta
    kk = bmm("gid,gjd->gij", jnp.concatenate([kb.astype(BF16), q], axis=1), k)
    neg_a = jnp.where(strict, -(kk[:, :c] * decay), 0.0)
    qk = (kk[:, c:] * decay).astype(BF16)
    mp = jnp.concatenate([neg_a, eye], axis=-1)
    right = lax.broadcasted_iota(jnp.int32, mp.shape, 2) >= c
    level = 1
    while level < c:
        mp = bmm("gij,gjk->gik", mp[:, :, :c].astype(BF16), mp.astype(BF16)) + jnp.where(right, mp, 0.0)
        level *= 2
    inv = mp[:, :, c:]
    rhs = jnp.concatenate([kb * eg, heads(v_ref).astype(F32) * beta], axis=-1)
    wu = bmm("gij,gjd->gid", inv.astype(BF16), rhs.astype(BF16))
    kd_t = jnp.swapaxes(kf * jnp.exp(g_last - gc_c), 1, 2).astype(BF16)
    lhs2 = jnp.concatenate([qk, kd_t], axis=1)
    lhs1 = jnp.concatenate([wu[:, :, :GDN_DK], q.astype(F32) * eg], axis=1).astype(BF16)
    u = wu[:, :, GDN_DK:]
    g_decay = jnp.exp(g_last)

    state = state_ref[...]
    outs = []
    for n in range(nc):
        gs = slice(n * per_chunk, (n + 1) * per_chunk)
        ws_qs = bmm("hid,hdv->hiv", lhs1[gs], state.astype(BF16))
        v_new = (u[gs] - ws_qs[:, :c]).astype(BF16)
        upd = bmm("hij,hjv->hiv", lhs2[gs], v_new)
        state = state * g_decay[gs] + upd[:, c:]
        outs.append(ws_qs[:, c:] + upd[:, :c])
    state_ref[...] = state

    for b in range(nb):
        for h in range(nh):
            hs = slice(h * GDN_DV, (h + 1) * GDN_DV)
            oh = jnp.concatenate([o[b * nh + h] for o in outs], axis=0)
            oh = oh * lax.rsqrt(jnp.mean(oh * oh, axis=-1, keepdims=True) + NORM_EPS) * ng_ref[...]
            o_ref[b, :, hs] = (oh * _silu(z_ref[b, :, hs])).astype(BF16)


def _attn_kernel(q_ref, k_ref, v_ref, o_ref):
    t = ATTN_ROWS
    nh = MLA_HEADS
    qi = pl.program_id(1)
    q = q_ref[...]
    bmm = functools.partial(jnp.einsum, preferred_element_type=F32)
    ones = jnp.ones((nh, ATTN_ONES_ROWS, t), BF16)

    def block(kb, carry, masked):
        m, acc = carry
        start = pl.multiple_of(kb * t, t)
        s = bmm("hkd,hqd->hkq", k_ref[:, pl.ds(start, t), :], q)
        if masked:
            kc = lax.broadcasted_iota(jnp.int32, s.shape, 1) // CHUNK
            qc = lax.broadcasted_iota(jnp.int32, s.shape, 2) // CHUNK
            s = jnp.where(kc <= qc, s, -jnp.inf)
        m_new = jnp.maximum(m, jnp.max(s, axis=1, keepdims=True))
        p = jnp.exp2(s - m_new).astype(BF16)
        v1 = jnp.concatenate([v_ref[:, kb], ones], axis=1)
        acc = jnp.exp2(m - m_new) * acc + bmm("hdk,hkq->hdq", v1, p)
        return m_new, acc

    init = (jnp.full((nh, 1, t), -jnp.inf, F32), jnp.zeros((nh, MLA_V + ATTN_ONES_ROWS, t), F32))
    carry = lax.fori_loop(0, qi, lambda kb, cr: block(kb, cr, False), init)
    _, acc = block(qi, carry, True)
    for h in range(nh):
        o_t = acc[h, :MLA_V] / acc[h, MLA_V:MLA_V + 1]
        o_ref[:, h * MLA_V:(h + 1) * MLA_V] = o_t.T.astype(BF16)


def _ffn_kernel(x_ref, oa_ref, ob_ref, p_ref, wo_ref, ln1g_ref, ln1b_ref, wup_ref, cw_ref, cb_ref,
                wdn_ref, wg_ref, bg_ref, wp_ref, ln2g_ref, ln2b_ref, out_ref, carry_ref, perm_ref, act_ref):
    ns = perm_ref.shape[0]
    rows = x_ref.shape[0] // ns
    streams = [slice(i * rows, (i + 1) * rows) for i in range(ns)]
    halo = (FFN_CONV - 1) * SUBLANES

    @pl.when(pl.program_id(1) == 0)
    def _():
        carry_ref[...] = jnp.zeros_like(carry_ref)

    mix = [_dot(oa_ref[s, :], wo_ref[:GDN_VW, :]) + _dot(ob_ref[s, :], wo_ref[GDN_VW:, :]) for s in streams]
    pp = [_dot(p_ref[s, :].astype(BF16), wp_ref[...]) for s in streams]
    h = [_layernorm(ALPHA * x_ref[s, :] + m, ln1g_ref[...], ln1b_ref[...]) for s, m in zip(streams, mix)]
    resid = [ALPHA * hi + _sigmoid(_dot(hi.astype(BF16), wg_ref[...]) + bg_ref[...]) * pi
             for hi, pi in zip(h, pp)]

    hp = []
    for i in range(ns):
        _interleave_rows(h[i], perm_ref.at[i])
        hp.append(jnp.concatenate([perm_ref[i, j] for j in range(D_MODEL // LANES)], axis=-1).astype(BF16))
    first = lax.broadcasted_iota(jnp.int32, (halo, FFN_COLS), 0) % SUBLANES == 0

    def up_conv(cols):
        prev_tail = carry_ref[:, cols]
        ys = []
        for i in range(ns):
            u = _dot(hp[i], wup_ref[:, cols])
            tail = u[rows - halo:]
            head = jnp.where(first, _roll_groups(prev_tail), _roll_groups(tail))
            ext = jnp.concatenate([head, u], axis=0)
            y = cb_ref[:, cols]
            for d in range(FFN_CONV):
                y = y + ext[halo - d * SUBLANES:halo - d * SUBLANES + rows] * cw_ref[FFN_CONV - 1 - d:FFN_CONV - d, cols]
            ys.append(y)
            prev_tail = tail
        carry_ref[:, cols] = prev_tail
        return ys

    for f in range(D_FF // FFN_COLS):
        gate = up_conv(slice(f * FFN_COLS, (f + 1) * FFN_COLS))
        up = up_conv(slice(D_FF + f * FFN_COLS, D_FF + (f + 1) * FFN_COLS))
        for i in range(ns):
            act_ref[i, :, f * FFN_COLS:(f + 1) * FFN_COLS] = (_silu(gate[i]) * up[i]).astype(BF16)
    for i in range(ns):
        ffn = _deinterleave_rows(_dot(act_ref[i], wdn_ref[...]), perm_ref.at[i])
        out_ref[streams[i], :] = _layernorm(resid[i] + ffn, ln2g_ref[...], ln2b_ref[...])


def _const_spec(shape):
    nd = len(shape)
    return pl.BlockSpec(shape, lambda *_: (0,) * nd, pipeline_mode=pl.Buffered(1))


def _params(*sem):
    return pltpu.CompilerParams(dimension_semantics=sem, vmem_limit_bytes=VMEM_LIMIT_BYTES)


def _swap_halves(w):
    half = w.shape[-1] // 2
    return jnp.concatenate([w[..., half:], w[..., :half]], axis=-1)


def _rope_table(seq):
    inv = ROPE_THETA ** (-jnp.arange(0, MLA_ROPE, 2, dtype=F32) / MLA_ROPE)
    ang = jnp.arange(seq, dtype=F32)[:, None] * inv[None, :]
    cos, sin = jnp.cos(ang), jnp.sin(ang)
    return jnp.concatenate([cos, cos, -sin, sin], axis=-1)


def _layer(x, p, w_in, gdn_conv_w, gdn_a_log, gdn_dt_bias, gdn_norm_g, mla_q_norm_g, mla_w_q_up,
           mla_kv_norm_g, mla_w_kv_up, w_out, ln1_g, ln1_b, ffn_w_up, ffn_conv_w, ffn_conv_b,
           ffn_w_down, ple_w_gate, ple_b_gate, ple_w_proj, ln2_g, ln2_b):
    bsz, seq, _ = x.shape
    assert seq % PROJ_ROWS == 0 and seq % (GDN_CHUNKS * CHUNK) == 0
    assert seq % ATTN_ROWS == 0 and seq % FFN_ROWS == 0 and D_FF % FFN_COLS == 0
    assert PROJ_ROWS == ATTN_ROWS

    o_z = 2 * GDN_QK + GDN_VW
    o_a = o_z + GDN_VW
    o_cq = o_a + 2 * GDN_HEADS
    o_ckv = o_cq + MLA_Q_LORA
    o_kr = o_ckv + MLA_KV_LORA
    w_kr = w_in[:, o_kr:o_kr + MLA_ROPE]
    w1 = jnp.concatenate([
        w_in[:, :o_a], w_in[:, o_cq:o_kr], w_kr, _swap_halves(w_kr), w_in[:, o_a:o_cq],
        jnp.zeros((D_MODEL, LANES - 2 * GDN_HEADS), F32)], axis=1).astype(BF16)
    assert w1.shape[1] == C_END
    gpar = jnp.zeros((2, LANES), F32)
    gpar = gpar.at[0, :GDN_HEADS].set(gdn_a_log).at[1, :GDN_HEADS].set(gdn_dt_bias)
    wq = mla_w_q_up.reshape(MLA_Q_LORA, MLA_HEADS, MLA_QKW)
    wq = jnp.concatenate([wq, _swap_halves(wq[..., MLA_NOPE:])], axis=-1)
    wq = wq.reshape(MLA_Q_LORA, MLA_HEADS * 2 * LANES).astype(BF16)
    wkv = mla_w_kv_up.reshape(MLA_KV_LORA, MLA_HEADS, MLA_NOPE + MLA_V)
    wvt = wkv[..., MLA_NOPE:].reshape(MLA_KV_LORA, -1).T.astype(BF16)
    wkv = wkv[..., :MLA_NOPE].reshape(MLA_KV_LORA, -1).astype(BF16)
    rope_tab = _rope_table(seq)

    wup = ffn_w_up.astype(BF16)
    wdn = ffn_w_down.astype(BF16)

    row = lambda v: v.reshape(1, -1)

    tm = PROJ_ROWS
    tok = lambda n: pl.BlockSpec((None, tm, n), lambda b, s: (b, s, 0))
    hd = lambda n: pl.BlockSpec((None, MLA_HEADS, tm, n), lambda b, s: (b, 0, s, 0))
    sd = jax.ShapeDtypeStruct
    gq, gk, gv, gz, gcol, grow, mq, mk, mv = pl.pallas_call(
        _proj_kernel,
        grid=(bsz, seq // tm),
        in_specs=[tok(D_MODEL), _const_spec(w1.shape), _const_spec(gdn_conv_w.shape),
                  _const_spec(gpar.shape), _const_spec((1, MLA_Q_LORA)), _const_spec(wq.shape),
                  _const_spec((1, MLA_KV_LORA)), _const_spec(wkv.shape), _const_spec(wvt.shape),
                  pl.BlockSpec((tm, LANES), lambda b, s: (s, 0))],
        out_specs=[tok(GDN_QK), tok(GDN_QK), tok(GDN_VW), tok(GDN_VW), tok(LANES),
                   pl.BlockSpec((None, SUBLANES, tm), lambda b, s: (b, 0, s)),
                   hd(MLA_QKW), hd(MLA_QKW),
                   pl.BlockSpec((None, MLA_HEADS, None, MLA_V, tm), lambda b, s: (b, 0, s, 0, 0))],
        out_shape=[sd((bsz, seq, GDN_QK), BF16), sd((bsz, seq, GDN_QK), BF16),
                   sd((bsz, seq, GDN_VW), BF16), sd((bsz, seq, GDN_VW), F32),
                   sd((bsz, seq, LANES), F32), sd((bsz, SUBLANES, seq), F32),
                   sd((bsz, MLA_HEADS, seq, MLA_QKW), BF16), sd((bsz, MLA_HEADS, seq, MLA_QKW), BF16),
                   sd((bsz, MLA_HEADS, seq // tm, MLA_V, tm), BF16)],
        scratch_shapes=[pltpu.VMEM(((GDN_CONV - 1) * SUBLANES, 2 * GDN_QK + GDN_VW), F32),
                        pltpu.VMEM((PROJ_STREAMS, D_MODEL // LANES, tm // PROJ_STREAMS, LANES), F32),
                        pltpu.VMEM((PROJ_STREAMS, 1, tm // PROJ_STREAMS, LANES), F32)],
        compiler_params=_params("parallel", "arbitrary"),
        name="proj",
    )(x, w1, gdn_conv_w, gpar, row(mla_q_norm_g), wq, row(mla_kv_norm_g), wkv, wvt, rope_tab)

    cs = GDN_CHUNKS * CHUNK
    gb = GDN_SEQS if bsz % GDN_SEQS == 0 else 1
    gtok = lambda n: pl.BlockSpec((gb, cs, n), lambda b, s: (b, s, 0))
    out_a = pl.pallas_call(
        _gdn_kernel,
        grid=(bsz // gb, seq // cs),
        in_specs=[gtok(GDN_QK), gtok(GDN_QK), gtok(GDN_VW), gtok(LANES),
                  pl.BlockSpec((gb, SUBLANES, cs), lambda b, s: (b, 0, s)), gtok(GDN_VW),
                  _const_spec((1, GDN_DV))],
        out_specs=gtok(GDN_VW),
        out_shape=sd((bsz, seq, GDN_VW), BF16),
        scratch_shapes=[pltpu.VMEM((gb * GDN_HEADS, GDN_DK, GDN_DV), F32)],
        compiler_params=_params("parallel", "arbitrary"),
        name="gdn",
    )(gq, gk, gv, gcol, grow, gz, row(gdn_norm_g))

    tq = ATTN_ROWS
    out_b = pl.pallas_call(
        _attn_kernel,
        grid=(bsz, seq // tq),
        in_specs=[pl.BlockSpec((None, MLA_HEADS, tq, MLA_QKW), lambda b, i: (b, 0, i, 0)),
                  pl.BlockSpec((None, MLA_HEADS, seq, MLA_QKW), lambda b, i: (b, 0, 0, 0)),
                  pl.BlockSpec((None, MLA_HEADS, seq // tq, MLA_V, tq), lambda b, i: (b, 0, 0, 0, 0))],
        out_specs=pl.BlockSpec((None, tq, MLA_HEADS * MLA_V), lambda b, i: (b, i, 0)),
        out_shape=sd((bsz, seq, MLA_HEADS * MLA_V), BF16),
        compiler_params=_params("parallel", "arbitrary"),
        name="attn",
    )(mq, mk, mv)

    tf = FFN_ROWS
    ftok = lambda n: pl.BlockSpec((None, tf, n), lambda b, s: (b, s, 0))
    out = pl.pallas_call(
        _ffn_kernel,
        grid=(bsz, seq // tf),
        in_specs=[ftok(D_MODEL), ftok(GDN_VW), ftok(MLA_HEADS * MLA_V), ftok(PLE_DIM),
                  _const_spec((GDN_VW + MLA_HEADS * MLA_V, D_MODEL)),
                  _const_spec((1, D_MODEL)), _const_spec((1, D_MODEL)),
                  _const_spec(wup.shape), _const_spec(ffn_conv_w.shape), _const_spec((1, 2 * D_FF)),
                  _const_spec(wdn.shape), _const_spec((D_MODEL, D_MODEL)), _const_spec((1, D_MODEL)),
                  _const_spec((PLE_DIM, D_MODEL)), _const_spec((1, D_MODEL)), _const_spec((1, D_MODEL))],
        out_specs=ftok(D_MODEL),
        out_shape=sd((bsz, seq, D_MODEL), F32),
        scratch_shapes=[pltpu.VMEM(((FFN_CONV - 1) * SUBLANES, 2 * D_FF), F32),
                        pltpu.VMEM((FFN_STREAMS, D_MODEL // LANES, tf // FFN_STREAMS, LANES), F32),
                        pltpu.VMEM((FFN_STREAMS, tf // FFN_STREAMS, D_FF), BF16)],
        compiler_params=_params("parallel", "arbitrary"),
        name="ffn",
    )(x, out_a, out_b, p, w_out.astype(BF16), row(ln1_g), row(ln1_b), wup, ffn_conv_w, row(ffn_conv_b), wdn,
      ple_w_gate.astype(BF16), row(ple_b_gate), ple_w_proj.astype(BF16), row(ln2_g), row(ln2_b))
    return out


def kernel(x, p, w_in, gdn_conv_w, gdn_a_log, gdn_dt_bias, gdn_norm_g, mla_q_norm_g, mla_w_q_up,
           mla_kv_norm_g, mla_w_kv_up, w_out, ln1_g, ln1_b, ffn_w_up, ffn_conv_w, ffn_conv_b,
           ffn_w_down, ple_w_gate, ple_b_gate, ple_w_proj, ln2_g, ln2_b):
    h = x
    for i in range(DEPTH):
        h = _layer(h, p[i], w_in[i], gdn_conv_w[i], gdn_a_log[i], gdn_dt_bias[i], gdn_norm_g[i],
                   mla_q_norm_g[i], mla_w_q_up[i], mla_kv_norm_g[i], mla_w_kv_up[i], w_out[i],
                   ln1_g[i], ln1_b[i], ffn_w_up[i], ffn_conv_w[i], ffn_conv_b[i], ffn_w_down[i],
                   ple_w_gate[i], ple_b_gate[i], ple_w_proj[i], ln2_g[i], ln2_b[i])
    return h
```

```python
import functools
import math

import jax
import jax.numpy as jnp
from jax import lax
from jax.experimental import pallas as pl
from jax.experimental.pallas import tpu as pltpu

F32 = jnp.float32
BF16 = jnp.bfloat16

D_MODEL = 1024
CHUNK = 64
PLE_DIM = 256
GDN_HEADS = 4
GDN_DK = 128
GDN_DV = 128
GDN_CONV = 4
MLA_HEADS = 4
MLA_NOPE = 128
MLA_ROPE = 64
MLA_V = 128
MLA_Q_LORA = 384
MLA_KV_LORA = 256
ROPE_THETA = 10000.0
D_FF = 2816
FFN_CONV = 3
DEPTH = 1
ALPHA = (2.0 * DEPTH) ** 0.25
NORM_EPS = 1e-6
GDN_QK = GDN_HEADS * GDN_DK
GDN_VW = GDN_HEADS * GDN_DV
MLA_QKW = MLA_NOPE + MLA_ROPE
MLA_QSCALE = MLA_QKW ** -0.5 * math.log2(math.e)

LANES = 128
SUBLANES = 8
VMEM_LIMIT_BYTES = 56 * 1024 * 1024

PROJ_ROWS = 512
PROJ_STREAMS = 1
GDN_CHUNKS = 8
GDN_SEQS = 2
ATTN_ROWS = 512
ATTN_ONES_ROWS = 16
FFN_ROWS = 512
FFN_COLS = 256
FFN_STREAMS = 2

C_QKV = 0
C_Z = C_QKV + 2 * GDN_QK + GDN_VW
C_CQ = C_Z + GDN_VW
C_CKV = C_CQ + MLA_Q_LORA
C_KR = C_CKV + MLA_KV_LORA
C_AB = C_KR + 2 * MLA_ROPE
C_END = C_AB + LANES


def _dot(a, b):
    return jnp.dot(a, b, preferred_element_type=F32)


def _dot_nt(a, b):
    return lax.dot_general(a, b, (((1,), (1,)), ((), ())), preferred_element_type=F32)


def _dot_tn(a, b):
    return lax.dot_general(a, b, (((0,), (0,)), ((), ())), preferred_element_type=F32)


def _sigmoid(x):
    return 1.0 / (1.0 + jnp.exp(-x))


def _silu(x):
    return x * _sigmoid(x)


def _softplus(x):
    return jnp.maximum(x, 0.0) + jnp.log1p(jnp.exp(-jnp.abs(x)))


def _layernorm(x, g, b):
    mu = jnp.mean(x, axis=-1, keepdims=True)
    xc = x - mu
    var = jnp.mean(xc * xc, axis=-1, keepdims=True)
    return xc * lax.rsqrt(var + NORM_EPS) * g + b


def _rmsnorm(x, g):
    return x * lax.rsqrt(jnp.mean(x * x, axis=-1, keepdims=True) + NORM_EPS) * g


def _causal_conv(ext, w_ref, cols, taps):
    y = ext * w_ref[taps - 1:taps, cols]
    for d in range(1, taps):
        y = y + pltpu.roll(ext, d, 0) * w_ref[taps - 1 - d:taps - d, cols]
    return y[SUBLANES:]


def _interleave_rows(a, perm_ref):
    r8 = a.shape[0] // SUBLANES
    for j in range(a.shape[1] // LANES):
        for k in range(SUBLANES):
            perm_ref[j, pl.ds(k, r8, stride=SUBLANES), :] = a[k * r8:(k + 1) * r8, j * LANES:(j + 1) * LANES]


def _deinterleave_rows(a, perm_ref):
    r8 = a.shape[0] // SUBLANES
    cols = []
    for j in range(a.shape[1] // LANES):
        perm_ref[j] = a[:, j * LANES:(j + 1) * LANES]
        cols.append(jnp.concatenate(
            [perm_ref[j, pl.ds(k, r8, stride=SUBLANES), :] for k in range(SUBLANES)], axis=0))
    return jnp.concatenate(cols, axis=-1)


def _roll_groups(a):
    return jnp.concatenate([pltpu.roll(a[g:g + SUBLANES], 1, 0) for g in range(0, a.shape[0], SUBLANES)],
                           axis=0)


def _proj_kernel(x_ref, w1_ref, convw_ref, gpar_ref, qng_ref, wq_ref, kvng_ref, wkv_ref, wvt_ref, rope_ref,
                 gq_ref, gk_ref, gv_ref, gz_ref, gcol_ref, grow_ref, mq_ref, mk_ref, mv_ref,
                 carry_ref, perm_ref, unperm_ref):
    ns = perm_ref.shape[0]
    rows = x_ref.shape[0] // ns
    streams = [slice(i * rows, (i + 1) * rows) for i in range(ns)]
    halo = (GDN_CONV - 1) * SUBLANES
    wide = 2 * LANES

    @pl.when(pl.program_id(1) == 0)
    def _():
        carry_ref[...] = jnp.zeros_like(carry_ref)

    x = [x_ref[s, :] for s in streams]
    xb = [xi.astype(BF16) for xi in x]

    xp = []
    for i in range(ns):
        _interleave_rows(x[i], perm_ref.at[i])
        xp.append(jnp.concatenate([perm_ref[i, j] for j in range(D_MODEL // LANES)], axis=-1).astype(BF16))
    first = lax.broadcasted_iota(jnp.int32, (halo, wide), 0) % SUBLANES == 0
    outs = (gq_ref, gk_ref, gv_ref)
    for part in range(3):
        for pair in range(GDN_QK // wide):
            cols = slice(part * GDN_QK + pair * wide, part * GDN_QK + (pair + 1) * wide)
            prev_tail = carry_ref[:, cols]
            for i in range(ns):
                pre = _dot(xp[i], w1_ref[:, cols])
                tail = pre[rows - halo:]
                head = jnp.where(first, _roll_groups(prev_tail), _roll_groups(tail))
                prev_tail = tail
                ext = jnp.concatenate([head, pre], axis=0)
                y = ext[halo:] * convw_ref[GDN_CONV - 1:GDN_CONV, cols]
                for d in range(1, GDN_CONV):
                    y = y + ext[halo - d * SUBLANES:halo - d * SUBLANES + rows] * convw_ref[GDN_CONV - 1 - d:GDN_CONV - d, cols]
                y = _silu(y)
                for j in range(wide // GDN_DK):
                    yh = y[:, j * GDN_DK:(j + 1) * GDN_DK]
                    if part < 2:
                        yh = yh * lax.rsqrt(jnp.sum(yh * yh, axis=-1, keepdims=True) + NORM_EPS)
                    if part == 0:
                        yh = yh * (GDN_DK ** -0.5)
                    h = pair * (wide // GDN_DK) + j
                    outs[part][streams[i], h * GDN_DK:(h + 1) * GDN_DK] = _deinterleave_rows(
                        yh, unperm_ref.at[i]).astype(BF16)
            carry_ref[:, cols] = prev_tail

    lane = lax.broadcasted_iota(jnp.int32, (rows, LANES), 1)
    rpos = lax.broadcasted_iota(jnp.int32, (rows, LANES), 0) & (CHUNK - 1)
    half = lane < MLA_ROPE
    for i in range(ns):
        s = streams[i]
        gz_ref[s, :] = _dot(xb[i], w1_ref[:, C_Z:C_Z + GDN_VW])

        kr_ab = _dot(xb[i], w1_ref[:, C_KR:C_KR + wide])
        ab = kr_ab[:, LANES:]
        cs = -jnp.exp(gpar_ref[0:1, :]) * _softplus(ab + gpar_ref[1:2, :])
        sh = 1
        while sh < CHUNK:
            cs = cs + jnp.where(rpos >= sh, pltpu.roll(cs, sh, 0), 0.0)
            sh *= 2
        gcol = jnp.where(lane < GDN_HEADS, cs, _sigmoid(ab))
        gcol_ref[s, :] = gcol
        grow_ref[:, s] = gcol.T[:SUBLANES]

        tab = rope_ref[s, :]

        def rope(r, tab=tab):
            t = r * tab
            return jnp.where(half, t + pltpu.roll(t, MLA_ROPE, 1), 0.0)

        krope = rope(kr_ab[:, :LANES])[:, :MLA_ROPE].astype(BF16)
        cq = _rmsnorm(_dot(xb[i], w1_ref[:, C_CQ:C_CQ + MLA_Q_LORA]), qng_ref[...]).astype(BF16)
        ckv = _rmsnorm(_dot(xb[i], w1_ref[:, C_CKV:C_CKV + MLA_KV_LORA]), kvng_ref[...]).astype(BF16)
        for h in range(MLA_HEADS):
            qh = _dot(cq, wq_ref[:, h * wide:(h + 1) * wide])
            qr = rope(qh[:, LANES:])[:, :MLA_ROPE]
            mq_ref[h, s, :] = (jnp.concatenate([qh[:, :LANES], qr], axis=-1) * MLA_QSCALE).astype(BF16)
        for pair in range(MLA_HEADS * MLA_NOPE // wide):
            kn = _dot(ckv, wkv_ref[:, pair * wide:(pair + 1) * wide]).astype(BF16)
            v_t = _dot_nt(wvt_ref[pair * wide:(pair + 1) * wide, :], ckv).astype(BF16)
            for j in range(wide // MLA_NOPE):
                h = pair * (wide // MLA_NOPE) + j
                mk_ref[h, s, :] = jnp.concatenate([kn[:, j * MLA_NOPE:(j + 1) * MLA_NOPE], krope], axis=-1)
                mv_ref[h, :, s] = v_t[j * MLA_V:(j + 1) * MLA_V, :]


def _gdn_kernel(q_ref, k_ref, v_ref, gcol_ref, grow_ref, z_ref, ng_ref, o_ref, state_ref):
    @pl.when(pl.program_id(1) == 0)
    def _():
        state_ref[...] = jnp.zeros_like(state_ref)

    c = CHUNK
    nh = GDN_HEADS
    nc = GDN_CHUNKS
    nb = q_ref.shape[0]
    per_chunk = nb * nh
    ng_all = nc * per_chunk
    ri = lax.broadcasted_iota(jnp.int32, (ng_all, c, c), 1)
    ci = lax.broadcasted_iota(jnp.int32, (ng_all, c, c), 2)
    incl = ri >= ci
    strict = ri > ci
    eye = jnp.where(ri == ci, 1.0, 0.0).astype(F32)
    bmm = functools.partial(jnp.einsum, preferred_element_type=F32)

    def problems(fn):
        return jnp.stack([fn(b, slice(n * c, (n + 1) * c), h)
                          for n in range(nc) for b in range(nb) for h in range(nh)])

    def heads(ref):
        return problems(lambda b, rs, h: ref[b, rs, h * GDN_DK:(h + 1) * GDN_DK])

    gc_c = problems(lambda b, rs, h: gcol_ref[b, rs, h:h + 1])
    beta = problems(lambda b, rs, h: gcol_ref[b, rs, nh + h:nh + h + 1])
    gc_r = problems(lambda b, rs, h: grow_ref[b, h:h + 1, rs])
    g_last = gc_r[:, :, c - 1:c]
    decay = jnp.where(incl, jnp.exp(gc_c - gc_r), 0.0)
    eg = jnp.exp(gc_c)
    q = heads(q_ref)
    k = heads(k_ref)
    kf = k.astype(F32)
    kb = kf * beta
    kk = bmm("gid,gjd->gij", jnp.concatenate([kb.astype(BF16), q], axis=1), k)
    neg_a = jnp.where(strict, -(kk[:, :c] * decay), 0.0)
    qk = (kk[:, c:] * decay).astype(BF16)
    mp = jnp.concatenate([neg_a, eye], axis=-1)
    right = lax.broadcasted_iota(jnp.int32, mp.shape, 2) >= c
    level = 1
    while level < c:
        mp = bmm("gij,gjk->gik", mp[:, :, :c].astype(BF16), mp.astype(BF16)) + jnp.where(right, mp, 0.0)
        level *= 2
    inv = mp[:, :, c:]
    rhs = jnp.concatenate([kb * eg, heads(v_ref).astype(F32) * beta], axis=-1)
    wu = bmm("gij,gjd->gid", inv.astype(BF16), rhs.astype(BF16))
    kd_t = jnp.swapaxes(kf * jnp.exp(g_last - gc_c), 1, 2).astype(BF16)
    lhs2 = jnp.concatenate([qk, kd_t], axis=1)
    lhs1 = jnp.concatenate([wu[:, :, :GDN_DK], q.astype(F32) * eg], axis=1).astype(BF16)
    u = wu[:, :, GDN_DK:]
    g_decay = jnp.exp(g_last)

    state = state_ref[...]
    outs = []
    for n in range(nc):
        gs = slice(n * per_chunk, (n + 1) * per_chunk)
        ws_qs = bmm("hid,hdv->hiv", lhs1[gs], state.astype(BF16))
        v_new = (u[gs] - ws_qs[:, :c]).astype(BF16)
        upd = bmm("hij,hjv->hiv", lhs2[gs], v_new)
        state = state * g_decay[gs] + upd[:, c:]
        outs.append(ws_qs[:, c:] + upd[:, :c])
    state_ref[...] = state

    for b in range(nb):
        for h in range(nh):
            hs = slice(h * GDN_DV, (h + 1) * GDN_DV)
            oh = jnp.concatenate([o[b * nh + h] for o in outs], axis=0)
            oh = oh * lax.rsqrt(jnp.mean(oh * oh, axis=-1, keepdims=True) + NORM_EPS) * ng_ref[...]
            o_ref[b, :, hs] = (oh * _silu(z_ref[b, :, hs])).astype(BF16)


def _attn_kernel(q_ref, k_ref, v_ref, o_ref):
    t = ATTN_ROWS
    half = t // 2
    nh = MLA_HEADS
    bmm = functools.partial(jnp.einsum, preferred_element_type=F32)

    def scores(keys, queries):
        return bmm("hkd,hqd->hkq", k_ref[:, keys, :], q_ref[:, queries, :])

    def chunk_mask(s):
        kc = lax.broadcasted_iota(jnp.int32, s.shape, 1) // CHUNK
        qc = lax.broadcasted_iota(jnp.int32, s.shape, 2) // CHUNK
        return jnp.where(kc <= qc, s, -jnp.inf)

    def update(m, acc, s, v_t):
        m_new = jnp.maximum(m, jnp.max(s, axis=1, keepdims=True))
        p = jnp.exp2(s - m_new).astype(BF16)
        v1 = jnp.concatenate([v_t, jnp.ones((nh, ATTN_ONES_ROWS, v_t.shape[2]), BF16)], axis=1)
        return m_new, jnp.exp2(m - m_new) * acc + bmm("hdk,hkq->hdq", v1, p)

    def tile(nkb):
        diag = (nkb - 1) * t
        lo, hi = slice(0, half), slice(half, t)
        first_keys = lambda kb: slice(kb * t, (kb + 1) * t) if kb < nkb - 1 else slice(diag, diag + half)
        m = jnp.full((nh, 1, t), -jnp.inf, F32)
        acc = jnp.zeros((nh, MLA_V + ATTN_ONES_ROWS, t), F32)
        s_next = scores(first_keys(0), slice(0, t))
        for kb in range(nkb - 1):
            s, s_next = s_next, scores(first_keys(kb + 1), slice(0, t))
            m, acc = update(m, acc, s, v_ref[:, kb])
        s_hi = scores(slice(diag + half, diag + t), hi)
        m, acc = update(m, acc, chunk_mask(s_next), v_ref[:, nkb - 1, :, lo])
        m_hi, acc_hi = update(m[:, :, hi], acc[:, :, hi], chunk_mask(s_hi), v_ref[:, nkb - 1, :, hi])
        acc = jnp.concatenate([acc[:, :, lo], acc_hi], axis=-1)
        for h in range(nh):
            o_t = acc[h, :MLA_V] / acc[h, MLA_V:MLA_V + 1]
            o_ref[:, h * MLA_V:(h + 1) * MLA_V] = o_t.T.astype(BF16)

    for nkb in range(1, k_ref.shape[1] // t + 1):
        pl.when(pl.program_id(1) == nkb - 1)(functools.partial(tile, nkb))


def _ffn_kernel(x_ref, oa_ref, ob_ref, p_ref, wo_ref, ln1g_ref, ln1b_ref, wup_ref, cw_ref, cb_ref,
                wdn_ref, wg_ref, bg_ref, wp_ref, ln2g_ref, ln2b_ref, out_ref, carry_ref, perm_ref, act_ref):
    ns = perm_ref.shape[0]
    rows = x_ref.shape[0] // ns
    streams = [slice(i * rows, (i + 1) * rows) for i in range(ns)]
    halo = (FFN_CONV - 1) * SUBLANES

    @pl.when(pl.program_id(1) == 0)
    def _():
        carry_ref[...] = jnp.zeros_like(carry_ref)

    mix = [_dot(oa_ref[s, :], wo_ref[:GDN_VW, :]) + _dot(ob_ref[s, :], wo_ref[GDN_VW:, :]) for s in streams]
    pp = [_dot(p_ref[s, :].astype(BF16), wp_ref[...]) for s in streams]
    h = [_layernorm(ALPHA * x_ref[s, :] + m, ln1g_ref[...], ln1b_ref[...]) for s, m in zip(streams, mix)]
    resid = [ALPHA * hi + _sigmoid(_dot(hi.astype(BF16), wg_ref[...]) + bg_ref[...]) * pi
             for hi, pi in zip(h, pp)]

    hp = []
    for i in range(ns):
        _interleave_rows(h[i], perm_ref.at[i])
        hp.append(jnp.concatenate([perm_ref[i, j] for j in range(D_MODEL // LANES)], axis=-1).astype(BF16))
    first = lax.broadcasted_iota(jnp.int32, (halo, FFN_COLS), 0) % SUBLANES == 0

    def up_conv(cols):
        prev_tail = carry_ref[:, cols]
        ys = []
        for i in range(ns):
            u = _dot(hp[i], wup_ref[:, cols])
            tail = u[rows - halo:]
            head = jnp.where(first, _roll_groups(prev_tail), _roll_groups(tail))
            ext = jnp.concatenate([head, u], axis=0)
            y = cb_ref[:, cols]
            for d in range(FFN_CONV):
                y = y + ext[halo - d * SUBLANES:halo - d * SUBLANES + rows] * cw_ref[FFN_CONV - 1 - d:FFN_CONV - d, cols]
            ys.append(y)
            prev_tail = tail
        carry_ref[:, cols] = prev_tail
        return ys

    for f in range(D_FF // FFN_COLS):
        gate = up_conv(slice(f * FFN_COLS, (f + 1) * FFN_COLS))
        up = up_conv(slice(D_FF + f * FFN_COLS, D_FF + (f + 1) * FFN_COLS))
        for i in range(ns):
            act_ref[i, :, f * FFN_COLS:(f + 1) * FFN_COLS] = (_silu(gate[i]) * up[i]).astype(BF16)
    for i in range(ns):
        ffn = _deinterleave_rows(_dot(act_ref[i], wdn_ref[...]), perm_ref.at[i])
        out_ref[streams[i], :] = _layernorm(resid[i] + ffn, ln2g_ref[...], ln2b_ref[...])


def _const_spec(shape):
    nd = len(shape)
    return pl.BlockSpec(shape, lambda *_: (0,) * nd, pipeline_mode=pl.Buffered(1))


def _params(*sem):
    return pltpu.CompilerParams(dimension_semantics=sem, vmem_limit_bytes=VMEM_LIMIT_BYTES)


def _swap_halves(w):
    half = w.shape[-1] // 2
    return jnp.concatenate([w[..., half:], w[..., :half]], axis=-1)


def _rope_table(seq):
    inv = ROPE_THETA ** (-jnp.arange(0, MLA_ROPE, 2, dtype=F32) / MLA_ROPE)
    ang = jnp.arange(seq, dtype=F32)[:, None] * inv[None, :]
    cos, sin = jnp.cos(ang), jnp.sin(ang)
    return jnp.concatenate([cos, cos, -sin, sin], axis=-1)


def _layer(x, p, w_in, gdn_conv_w, gdn_a_log, gdn_dt_bias, gdn_norm_g, mla_q_norm_g, mla_w_q_up,
           mla_kv_norm_g, mla_w_kv_up, w_out, ln1_g, ln1_b, ffn_w_up, ffn_conv_w, ffn_conv_b,
           ffn_w_down, ple_w_gate, ple_b_gate, ple_w_proj, ln2_g, ln2_b):
    bsz, seq, _ = x.shape
    assert seq % PROJ_ROWS == 0 and seq % (GDN_CHUNKS * CHUNK) == 0
    assert seq % ATTN_ROWS == 0 and seq % FFN_ROWS == 0 and D_FF % FFN_COLS == 0
    assert PROJ_ROWS == ATTN_ROWS

    o_z = 2 * GDN_QK + GDN_VW
    o_a = o_z + GDN_VW
    o_cq = o_a + 2 * GDN_HEADS
    o_ckv = o_cq + MLA_Q_LORA
    o_kr = o_ckv + MLA_KV_LORA
    w_kr = w_in[:, o_kr:o_kr + MLA_ROPE]
    w1 = jnp.concatenate([
        w_in[:, :o_a], w_in[:, o_cq:o_kr], w_kr, _swap_halves(w_kr), w_in[:, o_a:o_cq],
        jnp.zeros((D_MODEL, LANES - 2 * GDN_HEADS), F32)], axis=1).astype(BF16)
    assert w1.shape[1] == C_END
    gpar = jnp.zeros((2, LANES), F32)
    gpar = gpar.at[0, :GDN_HEADS].set(gdn_a_log).at[1, :GDN_HEADS].set(gdn_dt_bias)
    wq = mla_w_q_up.reshape(MLA_Q_LORA, MLA_HEADS, MLA_QKW)
    wq = jnp.concatenate([wq, _swap_halves(wq[..., MLA_NOPE:])], axis=-1)
    wq = wq.reshape(MLA_Q_LORA, MLA_HEADS * 2 * LANES).astype(BF16)
    wkv = mla_w_kv_up.reshape(MLA_KV_LORA, MLA_HEADS, MLA_NOPE + MLA_V)
    wvt = wkv[..., MLA_NOPE:].reshape(MLA_KV_LORA, -1).T.astype(BF16)
    wkv = wkv[..., :MLA_NOPE].reshape(MLA_KV_LORA, -1).astype(BF16)
    rope_tab = _rope_table(seq)

    wup = ffn_w_up.astype(BF16)
    wdn = ffn_w_down.astype(BF16)

    row = lambda v: v.reshape(1, -1)

    tm = PROJ_ROWS
    tok = lambda n: pl.BlockSpec((None, tm, n), lambda b, s: (b, s, 0))
    hd = lambda n: pl.BlockSpec((None, MLA_HEADS, tm, n), lambda b, s: (b, 0, s, 0))
    sd = jax.ShapeDtypeStruct
    gq, gk, gv, gz, gcol, grow, mq, mk, mv = pl.pallas_call(
        _proj_kernel,
        grid=(bsz, seq // tm),
        in_specs=[tok(D_MODEL), _const_spec(w1.shape), _const_spec(gdn_conv_w.shape),
                  _const_spec(gpar.shape), _const_spec((1, MLA_Q_LORA)), _const_spec(wq.shape),
                  _const_spec((1, MLA_KV_LORA)), _const_spec(wkv.shape), _const_spec(wvt.shape),
                  pl.BlockSpec((tm, LANES), lambda b, s: (s, 0))],
        out_specs=[tok(GDN_QK), tok(GDN_QK), tok(GDN_VW), tok(GDN_VW), tok(LANES),
                   pl.BlockSpec((None, SUBLANES, tm), lambda b, s: (b, 0, s)),
                   hd(MLA_QKW), hd(MLA_QKW),
                   pl.BlockSpec((None, MLA_HEADS, None, MLA_V, tm), lambda b, s: (b, 0, s, 0, 0))],
        out_shape=[sd((bsz, seq, GDN_QK), BF16), sd((bsz, seq, GDN_QK), BF16),
                   sd((bsz, seq, GDN_VW), BF16), sd((bsz, seq, GDN_VW), F32),
                   sd((bsz, seq, LANES), F32), sd((bsz, SUBLANES, seq), F32),
                   sd((bsz, MLA_HEADS, seq, MLA_QKW), BF16), sd((bsz, MLA_HEADS, seq, MLA_QKW), BF16),
                   sd((bsz, MLA_HEADS, seq // tm, MLA_V, tm), BF16)],
        scratch_shapes=[pltpu.VMEM(((GDN_CONV - 1) * SUBLANES, 2 * GDN_QK + GDN_VW), F32),
                        pltpu.VMEM((PROJ_STREAMS, D_MODEL // LANES, tm // PROJ_STREAMS, LANES), F32),
                        pltpu.VMEM((PROJ_STREAMS, 1, tm // PROJ_STREAMS, LANES), F32)],
        compiler_params=_params("parallel", "arbitrary"),
        name="proj",
    )(x, w1, gdn_conv_w, gpar, row(mla_q_norm_g), wq, row(mla_kv_norm_g), wkv, wvt, rope_tab)

    cs = GDN_CHUNKS * CHUNK
    gb = GDN_SEQS if bsz % GDN_SEQS == 0 else 1
    gtok = lambda n: pl.BlockSpec((gb, cs, n), lambda b, s: (b, s, 0))
    out_a = pl.pallas_call(
        _gdn_kernel,
        grid=(bsz // gb, seq // cs),
        in_specs=[gtok(GDN_QK), gtok(GDN_QK), gtok(GDN_VW), gtok(LANES),
                  pl.BlockSpec((gb, SUBLANES, cs), lambda b, s: (b, 0, s)), gtok(GDN_VW),
                  _const_spec((1, GDN_DV))],
        out_specs=gtok(GDN_VW),
        out_shape=sd((bsz, seq, GDN_VW), BF16),
        scratch_shapes=[pltpu.VMEM((gb * GDN_HEADS, GDN_DK, GDN_DV), F32)],
        compiler_params=_params("parallel", "arbitrary"),
        name="gdn",
    )(gq, gk, gv, gcol, grow, gz, row(gdn_norm_g))

    tq = ATTN_ROWS
    out_b = pl.pallas_call(
        _attn_kernel,
        grid=(bsz, seq // tq),
        in_specs=[pl.BlockSpec((None, MLA_HEADS, tq, MLA_QKW), lambda b, i: (b, 0, i, 0)),
                  pl.BlockSpec((None, MLA_HEADS, seq, MLA_QKW), lambda b, i: (b, 0, 0, 0)),
                  pl.BlockSpec((None, MLA_HEADS, seq // tq, MLA_V, tq), lambda b, i: (b, 0, 0, 0, 0))],
        out_specs=pl.BlockSpec((None, tq, MLA_HEADS * MLA_V), lambda b, i: (b, i, 0)),
        out_shape=sd((bsz, seq, MLA_HEADS * MLA_V), BF16),
        compiler_params=_params("parallel", "arbitrary"),
        name="attn",
    )(mq, mk, mv)

    tf = FFN_ROWS
    ftok = lambda n: pl.BlockSpec((None, tf, n), lambda b, s: (b, s, 0))
    out = pl.pallas_call(
        _ffn_kernel,
        grid=(bsz, seq // tf),
        in_specs=[ftok(D_MODEL), ftok(GDN_VW), ftok(MLA_HEADS * MLA_V), ftok(PLE_DIM),
                  _const_spec((GDN_VW + MLA_HEADS * MLA_V, D_MODEL)),
                  _const_spec((1, D_MODEL)), _const_spec((1, D_MODEL)),
                  _const_spec(wup.shape), _const_spec(ffn_conv_w.shape), _const_spec((1, 2 * D_FF)),
                  _const_spec(wdn.shape), _const_spec((D_MODEL, D_MODEL)), _const_spec((1, D_MODEL)),
                  _const_spec((PLE_DIM, D_MODEL)), _const_spec((1, D_MODEL)), _const_spec((1, D_MODEL))],
        out_specs=ftok(D_MODEL),
        out_shape=sd((bsz, seq, D_MODEL), F32),
        scratch_shapes=[pltpu.VMEM(((FFN_CONV - 1) * SUBLANES, 2 * D_FF), F32),
                        pltpu.VMEM((FFN_STREAMS, D_MODEL // LANES, tf // FFN_STREAMS, LANES), F32),
                        pltpu.VMEM((FFN_STREAMS, tf // FFN_STREAMS, D_FF), BF16)],
        compiler_params=_params("parallel", "arbitrary"),
        name="ffn",
    )(x, out_a, out_b, p, w_out.astype(BF16), row(ln1_g), row(ln1_b), wup, ffn_conv_w, row(ffn_conv_b), wdn,
      ple_w_gate.astype(BF16), row(ple_b_gate), ple_w_proj.astype(BF16), row(ln2_g), row(ln2_b))
    return out


def kernel(x, p, w_in, gdn_conv_w, gdn_a_log, gdn_dt_bias, gdn_norm_g, mla_q_norm_g, mla_w_q_up,
           mla_kv_norm_g, mla_w_kv_up, w_out, ln1_g, ln1_b, ffn_w_up, ffn_conv_w, ffn_conv_b,
           ffn_w_down, ple_w_gate, ple_b_gate, ple_w_proj, ln2_g, ln2_b):
    h = x
    for i in range(DEPTH):
        h = _layer(h, p[i], w_in[i], gdn_conv_w[i], gdn_a_log[i], gdn_dt_bias[i], gdn_norm_g[i],
                   mla_q_norm_g[i], mla_w_q_up[i], mla_kv_norm_g[i], mla_w_kv_up[i], w_out[i],
                   ln1_g[i], ln1_b[i], ffn_w_up[i], ffn_conv_w[i], ffn_conv_b[i], ffn_w_down[i],
                   ple_w_gate[i], ple_b_gate[i], ple_w_proj[i], ln2_g[i], ln2_b[i])
    return h
```

```python
import functools
import math

import jax
import jax.numpy as jnp
from jax import lax
from jax.experimental import pallas as pl
from jax.experimental.pallas import tpu as pltpu

F32 = jnp.float32
BF16 = jnp.bfloat16

D_MODEL = 1024
CHUNK = 64
PLE_DIM = 256
GDN_HEADS = 4
GDN_DK = 128
GDN_DV = 128
GDN_CONV = 4
MLA_HEADS = 4
MLA_NOPE = 128
MLA_ROPE = 64
MLA_V = 128
MLA_Q_LORA = 384
MLA_KV_LORA = 256
ROPE_THETA = 10000.0
D_FF = 2816
FFN_CONV = 3
DEPTH = 1
ALPHA = (2.0 * DEPTH) ** 0.25
NORM_EPS = 1e-6
GDN_QK = GDN_HEADS * GDN_DK
GDN_VW = GDN_HEADS * GDN_DV
MLA_QKW = MLA_NOPE + MLA_ROPE
MLA_QSCALE = MLA_QKW ** -0.5 * math.log2(math.e)

LANES = 128
SUBLANES = 8
VMEM_LIMIT_BYTES = 56 * 1024 * 1024

PROJ_ROWS = 1024
PROJ_STREAMS = 1
GDN_CHUNKS = 8
GDN_SEQS = 4
ATTN_ROWS = 512
ATTN_ONES_ROWS = 16
FFN_ROWS = 512
FFN_COLS = 256
FFN_STREAMS = 2

C_QKV = 0
C_Z = C_QKV + 2 * GDN_QK + GDN_VW
C_CQ = C_Z + GDN_VW
C_CKV = C_CQ + MLA_Q_LORA
C_KR = C_CKV + MLA_KV_LORA
C_AB = C_KR + 2 * MLA_ROPE
C_END = C_AB + LANES


def _dot(a, b):
    return jnp.dot(a, b, preferred_element_type=F32)


def _dot_nt(a, b):
    return lax.dot_general(a, b, (((1,), (1,)), ((), ())), preferred_element_type=F32)


def _dot_tn(a, b):
    return lax.dot_general(a, b, (((0,), (0,)), ((), ())), preferred_element_type=F32)


def _sigmoid(x):
    return 1.0 / (1.0 + jnp.exp(-x))


def _silu(x):
    return x * _sigmoid(x)


def _softplus(x):
    return jnp.maximum(x, 0.0) + jnp.log1p(jnp.exp(-jnp.abs(x)))


def _layernorm(x, g, b):
    mu = jnp.mean(x, axis=-1, keepdims=True)
    xc = x - mu
    var = jnp.mean(xc * xc, axis=-1, keepdims=True)
    return xc * lax.rsqrt(var + NORM_EPS) * g + b


def _rmsnorm(x, g):
    return x * lax.rsqrt(jnp.mean(x * x, axis=-1, keepdims=True) + NORM_EPS) * g


def _causal_conv(ext, w_ref, cols, taps):
    y = ext * w_ref[taps - 1:taps, cols]
    for d in range(1, taps):
        y = y + pltpu.roll(ext, d, 0) * w_ref[taps - 1 - d:taps - d, cols]
    return y[SUBLANES:]


def _interleave_rows(a, perm_ref):
    r8 = a.shape[0] // SUBLANES
    for j in range(a.shape[1] // LANES):
        for k in range(SUBLANES):
            perm_ref[j, pl.ds(k, r8, stride=SUBLANES), :] = a[k * r8:(k + 1) * r8, j * LANES:(j + 1) * LANES]


def _deinterleave_rows(a, perm_ref):
    r8 = a.shape[0] // SUBLANES
    cols = []
    for j in range(a.shape[1] // LANES):
        perm_ref[j] = a[:, j * LANES:(j + 1) * LANES]
        cols.append(jnp.concatenate(
            [perm_ref[j, pl.ds(k, r8, stride=SUBLANES), :] for k in range(SUBLANES)], axis=0))
    return jnp.concatenate(cols, axis=-1)


def _roll_groups(a):
    return jnp.concatenate([pltpu.roll(a[g:g + SUBLANES], 1, 0) for g in range(0, a.shape[0], SUBLANES)],
                           axis=0)


def _proj_kernel(x_ref, w1_ref, convw_ref, gpar_ref, qng_ref, wq_ref, kvng_ref, wkv_ref, wvt_ref, rope_ref,
                 gq_ref, gk_ref, gv_ref, gz_ref, gcol_ref, grow_ref, mq_ref, mk_ref, mv_ref,
                 carry_ref, perm_ref, unperm_ref):
    ns = perm_ref.shape[0]
    rows = x_ref.shape[0] // ns
    streams = [slice(i * rows, (i + 1) * rows) for i in range(ns)]
    halo = (GDN_CONV - 1) * SUBLANES
    wide = 2 * LANES

    @pl.when(pl.program_id(1) == 0)
    def _():
        carry_ref[...] = jnp.zeros_like(carry_ref)

    x = [x_ref[s, :] for s in streams]
    xb = [xi.astype(BF16) for xi in x]

    xp = []
    for i in range(ns):
        _interleave_rows(x[i], perm_ref.at[i])
        xp.append(jnp.concatenate([perm_ref[i, j] for j in range(D_MODEL // LANES)], axis=-1).astype(BF16))
    first = lax.broadcasted_iota(jnp.int32, (halo, wide), 0) % SUBLANES == 0
    outs = (gq_ref, gk_ref, gv_ref)
    for part in range(3):
        for pair in range(GDN_QK // wide):
            cols = slice(part * GDN_QK + pair * wide, part * GDN_QK + (pair + 1) * wide)
            prev_tail = carry_ref[:, cols]
            for i in range(ns):
                pre = _dot(xp[i], w1_ref[:, cols])
                tail = pre[rows - halo:]
                head = jnp.where(first, _roll_groups(prev_tail), _roll_groups(tail))
                prev_tail = tail
                ext = jnp.concatenate([head, pre], axis=0)
                y = ext[halo:] * convw_ref[GDN_CONV - 1:GDN_CONV, cols]
                for d in range(1, GDN_CONV):
                    y = y + ext[halo - d * SUBLANES:halo - d * SUBLANES + rows] * convw_ref[GDN_CONV - 1 - d:GDN_CONV - d, cols]
                y = _silu(y)
                for j in range(wide // GDN_DK):
                    yh = y[:, j * GDN_DK:(j + 1) * GDN_DK]
                    if part < 2:
                        yh = yh * lax.rsqrt(jnp.sum(yh * yh, axis=-1, keepdims=True) + NORM_EPS)
                    if part == 0:
                        yh = yh * (GDN_DK ** -0.5)
                    h = pair * (wide // GDN_DK) + j
                    outs[part][streams[i], h * GDN_DK:(h + 1) * GDN_DK] = _deinterleave_rows(
                        yh, unperm_ref.at[i]).astype(BF16)
            carry_ref[:, cols] = prev_tail

    lane = lax.broadcasted_iota(jnp.int32, (rows, LANES), 1)
    rpos = lax.broadcasted_iota(jnp.int32, (rows, LANES), 0) & (CHUNK - 1)
    half = lane < MLA_ROPE
    for i in range(ns):
        s = streams[i]
        gz_ref[s, :] = _dot(xb[i], w1_ref[:, C_Z:C_Z + GDN_VW])

        kr_ab = _dot(xb[i], w1_ref[:, C_KR:C_KR + wide])
        ab = kr_ab[:, LANES:]
        cs = -jnp.exp(gpar_ref[0:1, :]) * _softplus(ab + gpar_ref[1:2, :])
        sh = 1
        while sh < CHUNK:
            cs = cs + jnp.where(rpos >= sh, pltpu.roll(cs, sh, 0), 0.0)
            sh *= 2
        gcol = jnp.where(lane < GDN_HEADS, cs, _sigmoid(ab))
        gcol_ref[s, :] = gcol
        grow_ref[:, s] = gcol.T[:SUBLANES]

        tab = rope_ref[s, :]

        def rope(r, tab=tab):
            t = r * tab
            return jnp.where(half, t + pltpu.roll(t, MLA_ROPE, 1), 0.0)

        krope = rope(kr_ab[:, :LANES])[:, :MLA_ROPE].astype(BF16)
        cq = _rmsnorm(_dot(xb[i], w1_ref[:, C_CQ:C_CQ + MLA_Q_LORA]), qng_ref[...]).astype(BF16)
        ckv = _rmsnorm(_dot(xb[i], w1_ref[:, C_CKV:C_CKV + MLA_KV_LORA]), kvng_ref[...]).astype(BF16)
        for h in range(MLA_HEADS):
            qh = _dot(cq, wq_ref[:, h * wide:(h + 1) * wide])
            qr = rope(qh[:, LANES:])[:, :MLA_ROPE]
            mq_ref[h, s, :] = (jnp.concatenate([qh[:, :LANES], qr], axis=-1) * MLA_QSCALE).astype(BF16)
        for pair in range(MLA_HEADS * MLA_NOPE // wide):
            kn = _dot(ckv, wkv_ref[:, pair * wide:(pair + 1) * wide]).astype(BF16)
            v_t = _dot_nt(wvt_ref[pair * wide:(pair + 1) * wide, :], ckv).astype(BF16)
            for j in range(wide // MLA_NOPE):
                h = pair * (wide // MLA_NOPE) + j
                mk_ref[h, s, :] = jnp.concatenate([kn[:, j * MLA_NOPE:(j + 1) * MLA_NOPE], krope], axis=-1)
                for kb in range(rows // ATTN_ROWS):
                    mv_ref[h, i * (rows // ATTN_ROWS) + kb] = v_t[j * MLA_V:(j + 1) * MLA_V,
                                                                   kb * ATTN_ROWS:(kb + 1) * ATTN_ROWS]


def _gdn_kernel(q_ref, k_ref, v_ref, gcol_ref, grow_ref, z_ref, ng_ref, o_ref, state_ref):
    @pl.when(pl.program_id(1) == 0)
    def _():
        state_ref[...] = jnp.zeros_like(state_ref)

    c = CHUNK
    nh = GDN_HEADS
    nc = GDN_CHUNKS
    nb = q_ref.shape[0]
    per_chunk = nb * nh
    ng_all = nc * per_chunk
    ri = lax.broadcasted_iota(jnp.int32, (ng_all, c, c), 1)
    ci = lax.broadcasted_iota(jnp.int32, (ng_all, c, c), 2)
    incl = ri >= ci
    strict = ri > ci
    eye = jnp.where(ri == ci, 1.0, 0.0).astype(F32)
    bmm = functools.partial(jnp.einsum, preferred_element_type=F32)

    def problems(fn):
        return jnp.stack([fn(b, slice(n * c, (n + 1) * c), h)
                          for n in range(nc) for b in range(nb) for h in range(nh)])

    def heads(ref):
        return problems(lambda b, rs, h: ref[b, rs, h * GDN_DK:(h + 1) * GDN_DK])

    gc_c = problems(lambda b, rs, h: gcol_ref[b, rs, h:h + 1])
    beta = problems(lambda b, rs, h: gcol_ref[b, rs, nh + h:nh + h + 1])
    gc_r = problems(lambda b, rs, h: grow_ref[b, h:h + 1, rs])
    g_last = gc_r[:, :, c - 1:c]
    q = heads(q_ref)
    k = heads(k_ref)
    kk = bmm("gid,gjd->gij", jnp.concatenate([k, q], axis=1), k)
    decay = jnp.where(incl, jnp.exp(gc_c - gc_r), 0.0)
    eg = jnp.exp(gc_c)
    kf = k.astype(F32)
    kb = kf * beta
    neg_a = jnp.where(strict, -(kk[:, :c] * (beta * decay)), 0.0)
    qk = (kk[:, c:] * decay).astype(BF16)
    mp = jnp.concatenate([neg_a, eye], axis=-1)
    right = lax.broadcasted_iota(jnp.int32, mp.shape, 2) >= c
    level = 1
    while level < c:
        mp = bmm("gij,gjk->gik", mp[:, :, :c].astype(BF16), mp.astype(BF16)) + jnp.where(right, mp, 0.0)
        level *= 2
    inv = mp[:, :, c:]
    rhs = jnp.concatenate([kb * eg, heads(v_ref).astype(F32) * beta], axis=-1)
    wu = bmm("gij,gjd->gid", inv.astype(BF16), rhs.astype(BF16))
    kd_t = jnp.swapaxes(kf * jnp.exp(g_last - gc_c), 1, 2).astype(BF16)
    lhs2 = jnp.concatenate([qk, kd_t], axis=1)
    lhs1 = jnp.concatenate([wu[:, :, :GDN_DK], q.astype(F32) * eg], axis=1).astype(BF16)
    u = wu[:, :, GDN_DK:]
    g_decay = jnp.exp(g_last)

    state = state_ref[...]
    for n in range(nc):
        gs = slice(n * per_chunk, (n + 1) * per_chunk)
        rs = slice(n * c, (n + 1) * c)
        ws_qs = bmm("hid,hdv->hiv", lhs1[gs], state.astype(BF16))
        v_new = (u[gs] - ws_qs[:, :c]).astype(BF16)
        upd = bmm("hij,hjv->hiv", lhs2[gs], v_new)
        state = state * g_decay[gs] + upd[:, c:]
        o = ws_qs[:, c:] + upd[:, :c]
        o = o * lax.rsqrt(jnp.mean(o * o, axis=-1, keepdims=True) + NORM_EPS) * ng_ref[...]
        for b in range(nb):
            z = z_ref[b, rs, :]
            gated = jnp.concatenate([o[b * nh + h] for h in range(nh)], axis=-1) * _silu(z)
            o_ref[b, rs, :] = gated.astype(BF16)
    state_ref[...] = state


def _attn_kernel(q_ref, k_ref, v_ref, o_ref):
    t = ATTN_ROWS
    half = t // 2
    nh = MLA_HEADS
    bmm = functools.partial(jnp.einsum, preferred_element_type=F32)

    def scores(keys, queries):
        return bmm("hkd,hqd->hkq", k_ref[:, keys, :], q_ref[:, queries, :])

    def chunk_mask(s):
        kc = lax.broadcasted_iota(jnp.int32, s.shape, 1) // CHUNK
        qc = lax.broadcasted_iota(jnp.int32, s.shape, 2) // CHUNK
        return jnp.where(kc <= qc, s, -jnp.inf)

    def update(m, acc, s, v_t):
        m_new = jnp.maximum(m, jnp.max(s, axis=1, keepdims=True))
        p = jnp.exp2(s - m_new).astype(BF16)
        v1 = jnp.concatenate([v_t, jnp.ones((nh, ATTN_ONES_ROWS, v_t.shape[2]), BF16)], axis=1)
        return m_new, jnp.exp2(m - m_new) * acc + bmm("hdk,hkq->hdq", v1, p)

    def tile(nkb):
        diag = (nkb - 1) * t
        lo, hi = slice(0, half), slice(half, t)
        first_keys = lambda kb: slice(kb * t, (kb + 1) * t) if kb < nkb - 1 else slice(diag, diag + half)
        m = jnp.full((nh, 1, t), -jnp.inf, F32)
        acc = jnp.zeros((nh, MLA_V + ATTN_ONES_ROWS, t), F32)
        s_next = scores(first_keys(0), slice(diag, diag + t))
        for kb in range(nkb - 1):
            s, s_next = s_next, scores(first_keys(kb + 1), slice(diag, diag + t))
            m, acc = update(m, acc, s, v_ref[:, kb])
        s_hi = scores(slice(diag + half, diag + t), slice(diag + half, diag + t))
        m, acc = update(m, acc, chunk_mask(s_next), v_ref[:, nkb - 1, :, lo])
        m_hi, acc_hi = update(m[:, :, hi], acc[:, :, hi], chunk_mask(s_hi), v_ref[:, nkb - 1, :, hi])
        acc = jnp.concatenate([acc[:, :, lo], acc_hi], axis=-1)
        for h in range(nh):
            o_t = acc[h, :MLA_V] / acc[h, MLA_V:MLA_V + 1]
            o_ref[diag:diag + t, h * MLA_V:(h + 1) * MLA_V] = o_t.T.astype(BF16)

    n_tiles = k_ref.shape[1] // t

    def tile_pair(j):
        tile(j + 1)
        tile(n_tiles - j)

    for j in range(n_tiles // 2):
        pl.when(pl.program_id(1) == j)(functools.partial(tile_pair, j))


def _ffn_kernel(x_ref, oa_ref, ob_ref, p_ref, wo_ref, ln1g_ref, ln1b_ref, wup_ref, cw_ref, cb_ref,
                wdn_ref, wg_ref, bg_ref, wp_ref, ln2g_ref, ln2b_ref, out_ref, carry_ref, perm_ref, act_ref):
    ns = perm_ref.shape[0]
    rows = x_ref.shape[0] // ns
    streams = [slice(i * rows, (i + 1) * rows) for i in range(ns)]
    halo = (FFN_CONV - 1) * SUBLANES

    @pl.when(pl.program_id(1) == 0)
    def _():
        carry_ref[...] = jnp.zeros_like(carry_ref)

    mix = [_dot(oa_ref[s, :], wo_ref[:GDN_VW, :]) + _dot(ob_ref[s, :], wo_ref[GDN_VW:, :]) for s in streams]
    pp = [_dot(p_ref[s, :].astype(BF16), wp_ref[...]) for s in streams]
    h = [_layernorm(ALPHA * x_ref[s, :] + m, ln1g_ref[...], ln1b_ref[...]) for s, m in zip(streams, mix)]
    resid = [ALPHA * hi + _sigmoid(_dot(hi.astype(BF16), wg_ref[...]) + bg_ref[...]) * pi
             for hi, pi in zip(h, pp)]

    hp = []
    for i in range(ns):
        _interleave_rows(h[i], perm_ref.at[i])
        hp.append(jnp.concatenate([perm_ref[i, j] for j in range(D_MODEL // LANES)], axis=-1).astype(BF16))
    first = lax.broadcasted_iota(jnp.int32, (halo, FFN_COLS), 0) % SUBLANES == 0

    def up_conv(cols):
        prev_tail = carry_ref[:, cols]
        ys = []
        for i in range(ns):
            u = _dot(hp[i], wup_ref[:, cols])
            tail = u[rows - halo:]
            head = jnp.where(first, _roll_groups(prev_tail), _roll_groups(tail))
            ext = jnp.concatenate([head, u], axis=0)
            y = cb_ref[:, cols]
            for d in range(FFN_CONV):
                y = y + ext[halo - d * SUBLANES:halo - d * SUBLANES + rows] * cw_ref[FFN_CONV - 1 - d:FFN_CONV - d, cols]
            ys.append(y)
            prev_tail = tail
        carry_ref[:, cols] = prev_tail
        return ys

    for f in range(D_FF // FFN_COLS):
        gate = up_conv(slice(f * FFN_COLS, (f + 1) * FFN_COLS))
        up = up_conv(slice(D_FF + f * FFN_COLS, D_FF + (f + 1) * FFN_COLS))
        for i in range(ns):
            act_ref[i, :, f * FFN_COLS:(f + 1) * FFN_COLS] = (_silu(gate[i]) * up[i]).astype(BF16)
    for i in range(ns):
        ffn = _deinterleave_rows(_dot(act_ref[i], wdn_ref[...]), perm_ref.at[i])
        out_ref[streams[i], :] = _layernorm(resid[i] + ffn, ln2g_ref[...], ln2b_ref[...])


def _const_spec(shape):
    nd = len(shape)
    return pl.BlockSpec(shape, lambda *_: (0,) * nd, pipeline_mode=pl.Buffered(1))


def _params(*sem):
    return pltpu.CompilerParams(dimension_semantics=sem, vmem_limit_bytes=VMEM_LIMIT_BYTES)


def _swap_halves(w):
    half = w.shape[-1] // 2
    return jnp.concatenate([w[..., half:], w[..., :half]], axis=-1)


def _rope_table(seq):
    inv = ROPE_THETA ** (-jnp.arange(0, MLA_ROPE, 2, dtype=F32) / MLA_ROPE)
    ang = jnp.arange(seq, dtype=F32)[:, None] * inv[None, :]
    cos, sin = jnp.cos(ang), jnp.sin(ang)
    return jnp.concatenate([cos, cos, -sin, sin], axis=-1)


def _layer(x, p, w_in, gdn_conv_w, gdn_a_log, gdn_dt_bias, gdn_norm_g, mla_q_norm_g, mla_w_q_up,
           mla_kv_norm_g, mla_w_kv_up, w_out, ln1_g, ln1_b, ffn_w_up, ffn_conv_w, ffn_conv_b,
           ffn_w_down, ple_w_gate, ple_b_gate, ple_w_proj, ln2_g, ln2_b):
    bsz, seq, _ = x.shape
    assert seq % PROJ_ROWS == 0 and seq % (GDN_CHUNKS * CHUNK) == 0
    assert seq % (2 * ATTN_ROWS) == 0 and seq % FFN_ROWS == 0 and D_FF % FFN_COLS == 0
    assert PROJ_ROWS % (PROJ_STREAMS * ATTN_ROWS) == 0

    o_z = 2 * GDN_QK + GDN_VW
    o_a = o_z + GDN_VW
    o_cq = o_a + 2 * GDN_HEADS
    o_ckv = o_cq + MLA_Q_LORA
    o_kr = o_ckv + MLA_KV_LORA
    w_kr = w_in[:, o_kr:o_kr + MLA_ROPE]
    w1 = jnp.concatenate([
        w_in[:, :o_a], w_in[:, o_cq:o_kr], w_kr, _swap_halves(w_kr), w_in[:, o_a:o_cq],
        jnp.zeros((D_MODEL, LANES - 2 * GDN_HEADS), F32)], axis=1).astype(BF16)
    assert w1.shape[1] == C_END
    gpar = jnp.zeros((2, LANES), F32)
    gpar = gpar.at[0, :GDN_HEADS].set(gdn_a_log).at[1, :GDN_HEADS].set(gdn_dt_bias)
    wq = mla_w_q_up.reshape(MLA_Q_LORA, MLA_HEADS, MLA_QKW)
    wq = jnp.concatenate([wq, _swap_halves(wq[..., MLA_NOPE:])], axis=-1)
    wq = wq.reshape(MLA_Q_LORA, MLA_HEADS * 2 * LANES).astype(BF16)
    wkv = mla_w_kv_up.reshape(MLA_KV_LORA, MLA_HEADS, MLA_NOPE + MLA_V)
    wvt = wkv[..., MLA_NOPE:].reshape(MLA_KV_LORA, -1).T.astype(BF16)
    wkv = wkv[..., :MLA_NOPE].reshape(MLA_KV_LORA, -1).astype(BF16)
    rope_tab = _rope_table(seq)

    wup = ffn_w_up.astype(BF16)
    wdn = ffn_w_down.astype(BF16)

    row = lambda v: v.reshape(1, -1)

    tm = PROJ_ROWS
    tok = lambda n: pl.BlockSpec((None, tm, n), lambda b, s: (b, s, 0))
    hd = lambda n: pl.BlockSpec((None, MLA_HEADS, tm, n), lambda b, s: (b, 0, s, 0))
    sd = jax.ShapeDtypeStruct
    gq, gk, gv, gz, gcol, grow, mq, mk, mv = pl.pallas_call(
        _proj_kernel,
        grid=(bsz, seq // tm),
        in_specs=[tok(D_MODEL), _const_spec(w1.shape), _const_spec(gdn_conv_w.shape),
                  _const_spec(gpar.shape), _const_spec((1, MLA_Q_LORA)), _const_spec(wq.shape),
                  _const_spec((1, MLA_KV_LORA)), _const_spec(wkv.shape), _const_spec(wvt.shape),
                  pl.BlockSpec((tm, LANES), lambda b, s: (s, 0))],
        out_specs=[tok(GDN_QK), tok(GDN_QK), tok(GDN_VW), tok(GDN_VW), tok(LANES),
                   pl.BlockSpec((None, SUBLANES, tm), lambda b, s: (b, 0, s)),
                   hd(MLA_QKW), hd(MLA_QKW),
                   pl.BlockSpec((None, MLA_HEADS, tm // ATTN_ROWS, MLA_V, ATTN_ROWS),
                                lambda b, s: (b, 0, s, 0, 0))],
        out_shape=[sd((bsz, seq, GDN_QK), BF16), sd((bsz, seq, GDN_QK), BF16),
                   sd((bsz, seq, GDN_VW), BF16), sd((bsz, seq, GDN_VW), F32),
                   sd((bsz, seq, LANES), F32), sd((bsz, SUBLANES, seq), F32),
                   sd((bsz, MLA_HEADS, seq, MLA_QKW), BF16), sd((bsz, MLA_HEADS, seq, MLA_QKW), BF16),
                   sd((bsz, MLA_HEADS, seq // ATTN_ROWS, MLA_V, ATTN_ROWS), BF16)],
        scratch_shapes=[pltpu.VMEM(((GDN_CONV - 1) * SUBLANES, 2 * GDN_QK + GDN_VW), F32),
                        pltpu.VMEM((PROJ_STREAMS, D_MODEL // LANES, tm // PROJ_STREAMS, LANES), F32),
                        pltpu.VMEM((PROJ_STREAMS, 1, tm // PROJ_STREAMS, LANES), F32)],
        compiler_params=_params("parallel", "arbitrary"),
        name="proj",
    )(x, w1, gdn_conv_w, gpar, row(mla_q_norm_g), wq, row(mla_kv_norm_g), wkv, wvt, rope_tab)

    cs = GDN_CHUNKS * CHUNK
    gb = GDN_SEQS if bsz % GDN_SEQS == 0 else 1
    gtok = lambda n: pl.BlockSpec((gb, cs, n), lambda b, s: (b, s, 0))
    out_a = pl.pallas_call(
        _gdn_kernel,
        grid=(bsz // gb, seq // cs),
        in_specs=[gtok(GDN_QK), gtok(GDN_QK), gtok(GDN_VW), gtok(LANES),
                  pl.BlockSpec((gb, SUBLANES, cs), lambda b, s: (b, 0, s)), gtok(GDN_VW),
                  _const_spec((1, GDN_DV))],
        out_specs=gtok(GDN_VW),
        out_shape=sd((bsz, seq, GDN_VW), BF16),
        scratch_shapes=[pltpu.VMEM((gb * GDN_HEADS, GDN_DK, GDN_DV), F32)],
        compiler_params=_params("parallel", "arbitrary"),
        name="gdn",
    )(gq, gk, gv, gcol, grow, gz, row(gdn_norm_g))

    tq = ATTN_ROWS
    out_b = pl.pallas_call(
        _attn_kernel,
        grid=(bsz, seq // tq // 2),
        in_specs=[pl.BlockSpec((None, MLA_HEADS, seq, MLA_QKW), lambda b, i: (b, 0, 0, 0)),
                  pl.BlockSpec((None, MLA_HEADS, seq, MLA_QKW), lambda b, i: (b, 0, 0, 0)),
                  pl.BlockSpec((None, MLA_HEADS, seq // tq, MLA_V, tq), lambda b, i: (b, 0, 0, 0, 0))],
        out_specs=pl.BlockSpec((None, seq, MLA_HEADS * MLA_V), lambda b, i: (b, 0, 0)),
        out_shape=sd((bsz, seq, MLA_HEADS * MLA_V), BF16),
        compiler_params=_params("parallel", "arbitrary"),
        name="attn",
    )(mq, mk, mv)

    tf = FFN_ROWS
    ftok = lambda n: pl.BlockSpec((None, tf, n), lambda b, s: (b, s, 0))
    out = pl.pallas_call(
        _ffn_kernel,
        grid=(bsz, seq // tf),
        in_specs=[ftok(D_MODEL), ftok(GDN_VW), ftok(MLA_HEADS * MLA_V), ftok(PLE_DIM),
                  _const_spec((GDN_VW + MLA_HEADS * MLA_V, D_MODEL)),
                  _const_spec((1, D_MODEL)), _const_spec((1, D_MODEL)),
                  _const_spec(wup.shape), _const_spec(ffn_conv_w.shape), _const_spec((1, 2 * D_FF)),
                  _const_spec(wdn.shape), _const_spec((D_MODEL, D_MODEL)), _const_spec((1, D_MODEL)),
                  _const_spec((PLE_DIM, D_MODEL)), _const_spec((1, D_MODEL)), _const_spec((1, D_MODEL))],
        out_specs=ftok(D_MODEL),
        out_shape=sd((bsz, seq, D_MODEL), F32),
        scratch_shapes=[pltpu.VMEM(((FFN_CONV - 1) * SUBLANES, 2 * D_FF), F32),
                        pltpu.VMEM((FFN_STREAMS, D_MODEL // LANES, tf // FFN_STREAMS, LANES), F32),
                        pltpu.VMEM((FFN_STREAMS, tf // FFN_STREAMS, D_FF), BF16)],
        compiler_params=_params("parallel", "arbitrary"),
        name="ffn",
    )(x, out_a, out_b, p, w_out.astype(BF16), row(ln1_g), row(ln1_b), wup, ffn_conv_w, row(ffn_conv_b), wdn,
      ple_w_gate.astype(BF16), row(ple_b_gate), ple_w_proj.astype(BF16), row(ln2_g), row(ln2_b))
    return out


def kernel(x, p, w_in, gdn_conv_w, gdn_a_log, gdn_dt_bias, gdn_norm_g, mla_q_norm_g, mla_w_q_up,
           mla_kv_norm_g, mla_w_kv_up, w_out, ln1_g, ln1_b, ffn_w_up, ffn_conv_w, ffn_conv_b,
           ffn_w_down, ple_w_gate, ple_b_gate, ple_w_proj, ln2_g, ln2_b):
    h = x
    for i in range(DEPTH):
        h = _layer(h, p[i], w_in[i], gdn_conv_w[i], gdn_a_log[i], gdn_dt_bias[i], gdn_norm_g[i],
                   mla_q_norm_g[i], mla_w_q_up[i], mla_kv_norm_g[i], mla_w_kv_up[i], w_out[i],
                   ln1_g[i], ln1_b[i], ffn_w_up[i], ffn_conv_w[i], ffn_conv_b[i], ffn_w_down[i],
                   ple_w_gate[i], ple_b_gate[i], ple_w_proj[i], ln2_g[i], ln2_b[i])
    return h
```

```python
import functools
import math

import jax
import jax.numpy as jnp
from jax import lax
from jax.experimental import pallas as pl
from jax.experimental.pallas import tpu as pltpu

F32 = jnp.float32
BF16 = jnp.bfloat16

D_MODEL = 1024
CHUNK = 64
PLE_DIM = 256
GDN_HEADS = 4
GDN_DK = 128
GDN_DV = 128
GDN_CONV = 4
MLA_HEADS = 4
MLA_NOPE = 128
MLA_ROPE = 64
MLA_V = 128
MLA_Q_LORA = 384
MLA_KV_LORA = 256
ROPE_THETA = 10000.0
D_FF = 2816
FFN_CONV = 3
DEPTH = 1
ALPHA = (2.0 * DEPTH) ** 0.25
NORM_EPS = 1e-6
GDN_QK = GDN_HEADS * GDN_DK
GDN_VW = GDN_HEADS * GDN_DV
MLA_QKW = MLA_NOPE + MLA_ROPE
MLA_QSCALE = MLA_QKW ** -0.5 * math.log2(math.e)

LANES = 128
SUBLANES = 8
VMEM_LIMIT_BYTES = 56 * 1024 * 1024

PROJ_ROWS = 1024
PROJ_STREAMS = 1
GDN_CHUNKS = 8
GDN_SEQS = 4
ATTN_ROWS = 512
ATTN_ONES_ROWS = 16
FFN_ROWS = 512
FFN_COLS = 256
FFN_STREAMS = 2

C_QKV = 0
C_Z = C_QKV + 2 * GDN_QK + GDN_VW
C_CQ = C_Z + GDN_VW
C_CKV = C_CQ + MLA_Q_LORA
C_KR = C_CKV + MLA_KV_LORA
C_AB = C_KR + 2 * MLA_ROPE
C_END = C_AB + LANES


def _dot(a, b):
    return jnp.dot(a, b, preferred_element_type=F32)


def _dot_nt(a, b):
    return lax.dot_general(a, b, (((1,), (1,)), ((), ())), preferred_element_type=F32)


def _dot_tn(a, b):
    return lax.dot_general(a, b, (((0,), (0,)), ((), ())), preferred_element_type=F32)


def _sigmoid(x):
    return 1.0 / (1.0 + jnp.exp(-x))


def _silu(x):
    return x * _sigmoid(x)


def _softplus(x):
    return jnp.maximum(x, 0.0) + jnp.log1p(jnp.exp(-jnp.abs(x)))


def _layernorm(x, g, b):
    mu = jnp.mean(x, axis=-1, keepdims=True)
    xc = x - mu
    var = jnp.mean(xc * xc, axis=-1, keepdims=True)
    return xc * lax.rsqrt(var + NORM_EPS) * g + b


def _rmsnorm(x, g):
    return x * lax.rsqrt(jnp.mean(x * x, axis=-1, keepdims=True) + NORM_EPS) * g


def _causal_conv(ext, w_ref, cols, taps):
    y = ext * w_ref[taps - 1:taps, cols]
    for d in range(1, taps):
        y = y + pltpu.roll(ext, d, 0) * w_ref[taps - 1 - d:taps - d, cols]
    return y[SUBLANES:]


def _interleave_rows(a, perm_ref):
    r8 = a.shape[0] // SUBLANES
    for j in range(a.shape[1] // LANES):
        for k in range(SUBLANES):
            perm_ref[j, pl.ds(k, r8, stride=SUBLANES), :] = a[k * r8:(k + 1) * r8, j * LANES:(j + 1) * LANES]


def _deinterleave_rows(a, perm_ref):
    r8 = a.shape[0] // SUBLANES
    cols = []
    for j in range(a.shape[1] // LANES):
        perm_ref[j] = a[:, j * LANES:(j + 1) * LANES]
        cols.append(jnp.concatenate(
            [perm_ref[j, pl.ds(k, r8, stride=SUBLANES), :] for k in range(SUBLANES)], axis=0))
    return jnp.concatenate(cols, axis=-1)


def _roll_groups(a):
    return jnp.concatenate([pltpu.roll(a[g:g + SUBLANES], 1, 0) for g in range(0, a.shape[0], SUBLANES)],
                           axis=0)


def _proj_kernel(x_ref, w1_ref, convw_ref, gpar_ref, qng_ref, wq_ref, kvng_ref, wkv_ref, wvt_ref, rope_ref,
                 gq_ref, gk_ref, gv_ref, gz_ref, gcol_ref, grow_ref, mq_ref, mk_ref, mv_ref,
                 carry_ref, perm_ref, unperm_ref):
    ns = perm_ref.shape[0]
    rows = x_ref.shape[0] // ns
    streams = [slice(i * rows, (i + 1) * rows) for i in range(ns)]
    halo = (GDN_CONV - 1) * SUBLANES
    wide = 2 * LANES

    @pl.when(pl.program_id(1) == 0)
    def _():
        carry_ref[...] = jnp.zeros_like(carry_ref)

    x = [x_ref[s, :] for s in streams]
    xb = [xi.astype(BF16) for xi in x]

    xp = []
    for i in range(ns):
        _interleave_rows(x[i], perm_ref.at[i])
        xp.append(jnp.concatenate([perm_ref[i, j] for j in range(D_MODEL // LANES)], axis=-1).astype(BF16))
    first = lax.broadcasted_iota(jnp.int32, (halo, wide), 0) % SUBLANES == 0
    outs = (gq_ref, gk_ref, gv_ref)
    for part in range(3):
        for pair in range(GDN_QK // wide):
            cols = slice(part * GDN_QK + pair * wide, part * GDN_QK + (pair + 1) * wide)
            prev_tail = carry_ref[:, cols]
            for i in range(ns):
                pre = _dot(xp[i], w1_ref[:, cols])
                tail = pre[rows - halo:]
                head = jnp.where(first, _roll_groups(prev_tail), _roll_groups(tail))
                prev_tail = tail
                ext = jnp.concatenate([head, pre], axis=0)
                y = ext[halo:] * convw_ref[GDN_CONV - 1:GDN_CONV, cols]
                for d in range(1, GDN_CONV):
                    y = y + ext[halo - d * SUBLANES:halo - d * SUBLANES + rows] * convw_ref[GDN_CONV - 1 - d:GDN_CONV - d, cols]
                y = _silu(y)
                for j in range(wide // GDN_DK):
                    yh = y[:, j * GDN_DK:(j + 1) * GDN_DK]
                    if part < 2:
                        yh = yh * lax.rsqrt(jnp.sum(yh * yh, axis=-1, keepdims=True) + NORM_EPS)
                    if part == 0:
                        yh = yh * (GDN_DK ** -0.5)
                    h = pair * (wide // GDN_DK) + j
                    outs[part][streams[i], h * GDN_DK:(h + 1) * GDN_DK] = _deinterleave_rows(
                        yh, unperm_ref.at[i]).astype(BF16)
            carry_ref[:, cols] = prev_tail

    lane = lax.broadcasted_iota(jnp.int32, (rows, LANES), 1)
    rpos = lax.broadcasted_iota(jnp.int32, (rows, LANES), 0) & (CHUNK - 1)
    half = lane < MLA_ROPE
    for i in range(ns):
        s = streams[i]
        gz_ref[s, :] = _dot(xb[i], w1_ref[:, C_Z:C_Z + GDN_VW])

        kr_ab = _dot(xb[i], w1_ref[:, C_KR:C_KR + wide])
        ab = kr_ab[:, LANES:]
        cs = -jnp.exp(gpar_ref[0:1, :]) * _softplus(ab + gpar_ref[1:2, :])
        sh = 1
        while sh < CHUNK:
            cs = cs + jnp.where(rpos >= sh, pltpu.roll(cs, sh, 0), 0.0)
            sh *= 2
        gcol = jnp.where(lane < GDN_HEADS, cs, _sigmoid(ab))
        gcol_ref[s, :] = gcol
        grow_ref[:, s] = gcol.T[:SUBLANES]

        tab = rope_ref[s, :]

        def rope(r, tab=tab):
            t = r * tab
            return jnp.where(half, t + pltpu.roll(t, MLA_ROPE, 1), 0.0)

        krope = rope(kr_ab[:, :LANES])[:, :MLA_ROPE].astype(BF16)
        cq = _rmsnorm(_dot(xb[i], w1_ref[:, C_CQ:C_CQ + MLA_Q_LORA]), qng_ref[...]).astype(BF16)
        ckv = _rmsnorm(_dot(xb[i], w1_ref[:, C_CKV:C_CKV + MLA_KV_LORA]), kvng_ref[...]).astype(BF16)
        for h in range(MLA_HEADS):
            qh = _dot(cq, wq_ref[:, h * wide:(h + 1) * wide])
            qr = rope(qh[:, LANES:])[:, :MLA_ROPE]
            mq_ref[h, s, :] = (jnp.concatenate([qh[:, :LANES], qr], axis=-1) * MLA_QSCALE).astype(BF16)
        for pair in range(MLA_HEADS * MLA_NOPE // wide):
            kn = _dot(ckv, wkv_ref[:, pair * wide:(pair + 1) * wide]).astype(BF16)
            v_t = _dot_nt(wvt_ref[pair * wide:(pair + 1) * wide, :], ckv).astype(BF16)
            for j in range(wide // MLA_NOPE):
                h = pair * (wide // MLA_NOPE) + j
                mk_ref[h, s, :] = jnp.concatenate([kn[:, j * MLA_NOPE:(j + 1) * MLA_NOPE], krope], axis=-1)
                for kb in range(rows // ATTN_ROWS):
                    mv_ref[h, i * (rows // ATTN_ROWS) + kb] = v_t[j * MLA_V:(j + 1) * MLA_V,
                                                                   kb * ATTN_ROWS:(kb + 1) * ATTN_ROWS]


def _gdn_kernel(q_ref, k_ref, v_ref, gcol_ref, grow_ref, z_ref, ng_ref, o_ref, state_ref):
    @pl.when(pl.program_id(1) == 0)
    def _():
        state_ref[...] = jnp.zeros_like(state_ref)

    c = CHUNK
    nh = GDN_HEADS
    nc = GDN_CHUNKS
    nb = q_ref.shape[0]
    per_chunk = nb * nh
    ng_all = nc * per_chunk
    ri = lax.broadcasted_iota(jnp.int32, (ng_all, c, c), 1)
    ci = lax.broadcasted_iota(jnp.int32, (ng_all, c, c), 2)
    incl = ri >= ci
    strict = ri > ci
    eye = jnp.where(ri == ci, 1.0, 0.0).astype(F32)
    bmm = functools.partial(jnp.einsum, preferred_element_type=F32)

    def problems(fn):
        return jnp.stack([fn(b, slice(n * c, (n + 1) * c), h)
                          for n in range(nc) for b in range(nb) for h in range(nh)])

    def heads(ref):
        return problems(lambda b, rs, h: ref[b, rs, h * GDN_DK:(h + 1) * GDN_DK])

    gc_c = problems(lambda b, rs, h: gcol_ref[b, rs, h:h + 1])
    beta = problems(lambda b, rs, h: gcol_ref[b, rs, nh + h:nh + h + 1])
    gc_r = problems(lambda b, rs, h: grow_ref[b, h:h + 1, rs])
    g_last = gc_r[:, :, c - 1:c]
    q = heads(q_ref)
    k = heads(k_ref)
    kk = bmm("gid,gjd->gij", jnp.concatenate([k, q], axis=1), k)
    decay = jnp.where(incl, jnp.exp(gc_c - gc_r), 0.0)
    eg = jnp.exp(gc_c)
    kf = k.astype(F32)
    kb = kf * beta
    neg_a = jnp.where(strict, -(kk[:, :c] * (beta * decay)), 0.0)
    qk = (kk[:, c:] * decay).astype(BF16)
    mp = jnp.concatenate([neg_a, eye], axis=-1)
    right = lax.broadcasted_iota(jnp.int32, mp.shape, 2) >= c
    level = 1
    while level < c:
        mp = bmm("gij,gjk->gik", mp[:, :, :c].astype(BF16), mp.astype(BF16)) + jnp.where(right, mp, 0.0)
        level *= 2
    inv = mp[:, :, c:]
    rhs = jnp.concatenate([kb * eg, heads(v_ref).astype(F32) * beta], axis=-1)
    wu = bmm("gij,gjd->gid", inv.astype(BF16), rhs.astype(BF16))
    kd_t = jnp.swapaxes(kf * jnp.exp(g_last - gc_c), 1, 2).astype(BF16)
    lhs2 = jnp.concatenate([qk, kd_t], axis=1)
    lhs1 = jnp.concatenate([wu[:, :, :GDN_DK], q.astype(F32) * eg], axis=1).astype(BF16)
    u = wu[:, :, GDN_DK:]
    g_decay = jnp.exp(g_last)

    state = state_ref[...]
    for n in range(nc):
        gs = slice(n * per_chunk, (n + 1) * per_chunk)
        rs = slice(n * c, (n + 1) * c)
        ws_qs = bmm("hid,hdv->hiv", lhs1[gs], state.astype(BF16))
        v_new = (u[gs] - ws_qs[:, :c]).astype(BF16)
        upd = bmm("hij,hjv->hiv", lhs2[gs], v_new)
        state = state * g_decay[gs] + upd[:, c:]
        o = ws_qs[:, c:] + upd[:, :c]
        o = o * lax.rsqrt(jnp.mean(o * o, axis=-1, keepdims=True) + NORM_EPS) * ng_ref[...]
        for b in range(nb):
            z = z_ref[b, rs, :]
            gated = jnp.concatenate([o[b * nh + h] for h in range(nh)], axis=-1) * _silu(z)
            o_ref[b, rs, :] = gated.astype(BF16)
    state_ref[...] = state


def _attn_kernel(q_ref, k_ref, v_ref, o_ref):
    t = ATTN_ROWS
    half = t // 2
    nh = MLA_HEADS
    bmm = functools.partial(jnp.einsum, preferred_element_type=F32)

    def scores(keys, queries):
        return bmm("hkd,hqd->hkq", k_ref[:, keys, :], q_ref[:, queries, :])

    def chunk_mask(s):
        kc = lax.broadcasted_iota(jnp.int32, s.shape, 1) // CHUNK
        qc = lax.broadcasted_iota(jnp.int32, s.shape, 2) // CHUNK
        return jnp.where(kc <= qc, s, -jnp.inf)

    def update(m, acc, s, v_t):
        m_new = jnp.maximum(m, jnp.max(s, axis=1, keepdims=True))
        p = jnp.exp2(s - m_new).astype(BF16)
        v1 = jnp.concatenate([v_t, jnp.ones((nh, ATTN_ONES_ROWS, v_t.shape[2]), BF16)], axis=1)
        return m_new, jnp.exp2(m - m_new) * acc + bmm("hdk,hkq->hdq", v1, p)

    def tile(nkb):
        diag = (nkb - 1) * t
        lo, hi = slice(0, half), slice(half, t)
        first_keys = lambda kb: slice(kb * t, (kb + 1) * t) if kb < nkb - 1 else slice(diag, diag + half)
        m = jnp.full((nh, 1, t), -jnp.inf, F32)
        acc = jnp.zeros((nh, MLA_V + ATTN_ONES_ROWS, t), F32)
        s_next = scores(first_keys(0), slice(diag, diag + t))
        for kb in range(nkb - 1):
            s, s_next = s_next, scores(first_keys(kb + 1), slice(diag, diag + t))
            m, acc = update(m, acc, s, v_ref[:, kb])
        s_hi = scores(slice(diag + half, diag + t), slice(diag + half, diag + t))
        m, acc = update(m, acc, chunk_mask(s_next), v_ref[:, nkb - 1, :, lo])
        m_hi, acc_hi = update(m[:, :, hi], acc[:, :, hi], chunk_mask(s_hi), v_ref[:, nkb - 1, :, hi])
        acc = jnp.concatenate([acc[:, :, lo], acc_hi], axis=-1)
        for h in range(nh):
            o_t = acc[h, :MLA_V] / acc[h, MLA_V:MLA_V + 1]
            o_ref[diag:diag + t, h * MLA_V:(h + 1) * MLA_V] = o_t.T.astype(BF16)

    for nkb in range(k_ref.shape[1] // t, 0, -1):
        tile(nkb)


def _ffn_kernel(x_ref, oa_ref, ob_ref, p_ref, wo_ref, ln1g_ref, ln1b_ref, wup_ref, cw_ref, cb_ref,
                wdn_ref, wg_ref, bg_ref, wp_ref, ln2g_ref, ln2b_ref, out_ref, carry_ref, perm_ref, act_ref):
    ns = perm_ref.shape[0]
    rows = x_ref.shape[0] // ns
    streams = [slice(i * rows, (i + 1) * rows) for i in range(ns)]
    halo = (FFN_CONV - 1) * SUBLANES

    @pl.when(pl.program_id(1) == 0)
    def _():
        carry_ref[...] = jnp.zeros_like(carry_ref)

    mix = [_dot(oa_ref[s, :], wo_ref[:GDN_VW, :]) + _dot(ob_ref[s, :], wo_ref[GDN_VW:, :]) for s in streams]
    pp = [_dot(p_ref[s, :].astype(BF16), wp_ref[...]) for s in streams]
    h = [_layernorm(ALPHA * x_ref[s, :] + m, ln1g_ref[...], ln1b_ref[...]) for s, m in zip(streams, mix)]
    resid = [ALPHA * hi + _sigmoid(_dot(hi.astype(BF16), wg_ref[...]) + bg_ref[...]) * pi
             for hi, pi in zip(h, pp)]

    hp = []
    for i in range(ns):
        _interleave_rows(h[i], perm_ref.at[i])
        hp.append(jnp.concatenate([perm_ref[i, j] for j in range(D_MODEL // LANES)], axis=-1).astype(BF16))
    first = lax.broadcasted_iota(jnp.int32, (halo, FFN_COLS), 0) % SUBLANES == 0

    def up_conv(cols):
        prev_tail = carry_ref[:, cols]
        ys = []
        for i in range(ns):
            u = _dot(hp[i], wup_ref[:, cols])
            tail = u[rows - halo:]
            head = jnp.where(first, _roll_groups(prev_tail), _roll_groups(tail))
            ext = jnp.concatenate([head, u], axis=0)
            y = cb_ref[:, cols]
            for d in range(FFN_CONV):
                y = y + ext[halo - d * SUBLANES:halo - d * SUBLANES + rows] * cw_ref[FFN_CONV - 1 - d:FFN_CONV - d, cols]
            ys.append(y)
            prev_tail = tail
        carry_ref[:, cols] = prev_tail
        return ys

    for f in range(D_FF // FFN_COLS):
        gate = up_conv(slice(f * FFN_COLS, (f + 1) * FFN_COLS))
        up = up_conv(slice(D_FF + f * FFN_COLS, D_FF + (f + 1) * FFN_COLS))
        for i in range(ns):
            act_ref[i, :, f * FFN_COLS:(f + 1) * FFN_COLS] = (_silu(gate[i]) * up[i]).astype(BF16)
    for i in range(ns):
        ffn = _deinterleave_rows(_dot(act_ref[i], wdn_ref[...]), perm_ref.at[i])
        out_ref[streams[i], :] = _layernorm(resid[i] + ffn, ln2g_ref[...], ln2b_ref[...])


def _const_spec(shape):
    nd = len(shape)
    return pl.BlockSpec(shape, lambda *_: (0,) * nd, pipeline_mode=pl.Buffered(1))


def _params(*sem):
    return pltpu.CompilerParams(dimension_semantics=sem, vmem_limit_bytes=VMEM_LIMIT_BYTES)


def _swap_halves(w):
    half = w.shape[-1] // 2
    return jnp.concatenate([w[..., half:], w[..., :half]], axis=-1)


def _rope_table(seq):
    inv = ROPE_THETA ** (-jnp.arange(0, MLA_ROPE, 2, dtype=F32) / MLA_ROPE)
    ang = jnp.arange(seq, dtype=F32)[:, None] * inv[None, :]
    cos, sin = jnp.cos(ang), jnp.sin(ang)
    return jnp.concatenate([cos, cos, -sin, sin], axis=-1)


def _layer(x, p, w_in, gdn_conv_w, gdn_a_log, gdn_dt_bias, gdn_norm_g, mla_q_norm_g, mla_w_q_up,
           mla_kv_norm_g, mla_w_kv_up, w_out, ln1_g, ln1_b, ffn_w_up, ffn_conv_w, ffn_conv_b,
           ffn_w_down, ple_w_gate, ple_b_gate, ple_w_proj, ln2_g, ln2_b):
    bsz, seq, _ = x.shape
    assert seq % PROJ_ROWS == 0 and seq % (GDN_CHUNKS * CHUNK) == 0
    assert seq % ATTN_ROWS == 0 and seq % FFN_ROWS == 0 and D_FF % FFN_COLS == 0
    assert PROJ_ROWS % (PROJ_STREAMS * ATTN_ROWS) == 0

    o_z = 2 * GDN_QK + GDN_VW
    o_a = o_z + GDN_VW
    o_cq = o_a + 2 * GDN_HEADS
    o_ckv = o_cq + MLA_Q_LORA
    o_kr = o_ckv + MLA_KV_LORA
    w_kr = w_in[:, o_kr:o_kr + MLA_ROPE]
    w1 = jnp.concatenate([
        w_in[:, :o_a], w_in[:, o_cq:o_kr], w_kr, _swap_halves(w_kr), w_in[:, o_a:o_cq],
        jnp.zeros((D_MODEL, LANES - 2 * GDN_HEADS), F32)], axis=1).astype(BF16)
    assert w1.shape[1] == C_END
    gpar = jnp.zeros((2, LANES), F32)
    gpar = gpar.at[0, :GDN_HEADS].set(gdn_a_log).at[1, :GDN_HEADS].set(gdn_dt_bias)
    wq = mla_w_q_up.reshape(MLA_Q_LORA, MLA_HEADS, MLA_QKW)
    wq = jnp.concatenate([wq, _swap_halves(wq[..., MLA_NOPE:])], axis=-1)
    wq = wq.reshape(MLA_Q_LORA, MLA_HEADS * 2 * LANES).astype(BF16)
    wkv = mla_w_kv_up.reshape(MLA_KV_LORA, MLA_HEADS, MLA_NOPE + MLA_V)
    wvt = wkv[..., MLA_NOPE:].reshape(MLA_KV_LORA, -1).T.astype(BF16)
    wkv = wkv[..., :MLA_NOPE].reshape(MLA_KV_LORA, -1).astype(BF16)
    rope_tab = _rope_table(seq)

    wup = ffn_w_up.astype(BF16)
    wdn = ffn_w_down.astype(BF16)

    row = lambda v: v.reshape(1, -1)

    tm = PROJ_ROWS
    tok = lambda n: pl.BlockSpec((None, tm, n), lambda b, s: (b, s, 0))
    hd = lambda n: pl.BlockSpec((None, MLA_HEADS, tm, n), lambda b, s: (b, 0, s, 0))
    sd = jax.ShapeDtypeStruct
    gq, gk, gv, gz, gcol, grow, mq, mk, mv = pl.pallas_call(
        _proj_kernel,
        grid=(bsz, seq // tm),
        in_specs=[tok(D_MODEL), _const_spec(w1.shape), _const_spec(gdn_conv_w.shape),
                  _const_spec(gpar.shape), _const_spec((1, MLA_Q_LORA)), _const_spec(wq.shape),
                  _const_spec((1, MLA_KV_LORA)), _const_spec(wkv.shape), _const_spec(wvt.shape),
                  pl.BlockSpec((tm, LANES), lambda b, s: (s, 0))],
        out_specs=[tok(GDN_QK), tok(GDN_QK), tok(GDN_VW), tok(GDN_VW), tok(LANES),
                   pl.BlockSpec((None, SUBLANES, tm), lambda b, s: (b, 0, s)),
                   hd(MLA_QKW), hd(MLA_QKW),
                   pl.BlockSpec((None, MLA_HEADS, tm // ATTN_ROWS, MLA_V, ATTN_ROWS),
                                lambda b, s: (b, 0, s, 0, 0))],
        out_shape=[sd((bsz, seq, GDN_QK), BF16), sd((bsz, seq, GDN_QK), BF16),
                   sd((bsz, seq, GDN_VW), BF16), sd((bsz, seq, GDN_VW), F32),
                   sd((bsz, seq, LANES), F32), sd((bsz, SUBLANES, seq), F32),
                   sd((bsz, MLA_HEADS, seq, MLA_QKW), BF16), sd((bsz, MLA_HEADS, seq, MLA_QKW), BF16),
                   sd((bsz, MLA_HEADS, seq // ATTN_ROWS, MLA_V, ATTN_ROWS), BF16)],
        scratch_shapes=[pltpu.VMEM(((GDN_CONV - 1) * SUBLANES, 2 * GDN_QK + GDN_VW), F32),
                        pltpu.VMEM((PROJ_STREAMS, D_MODEL // LANES, tm // PROJ_STREAMS, LANES), F32),
                        pltpu.VMEM((PROJ_STREAMS, 1, tm // PROJ_STREAMS, LANES), F32)],
        compiler_params=_params("parallel", "arbitrary"),
        name="proj",
    )(x, w1, gdn_conv_w, gpar, row(mla_q_norm_g), wq, row(mla_kv_norm_g), wkv, wvt, rope_tab)

    cs = GDN_CHUNKS * CHUNK
    gb = GDN_SEQS if bsz % GDN_SEQS == 0 else 1
    gtok = lambda n: pl.BlockSpec((gb, cs, n), lambda b, s: (b, s, 0))
    out_a = pl.pallas_call(
        _gdn_kernel,
        grid=(bsz // gb, seq // cs),
        in_specs=[gtok(GDN_QK), gtok(GDN_QK), gtok(GDN_VW), gtok(LANES),
                  pl.BlockSpec((gb, SUBLANES, cs), lambda b, s: (b, 0, s)), gtok(GDN_VW),
                  _const_spec((1, GDN_DV))],
        out_specs=gtok(GDN_VW),
        out_shape=sd((bsz, seq, GDN_VW), BF16),
        scratch_shapes=[pltpu.VMEM((gb * GDN_HEADS, GDN_DK, GDN_DV), F32)],
        compiler_params=_params("parallel", "arbitrary"),
        name="gdn",
    )(gq, gk, gv, gcol, grow, gz, row(gdn_norm_g))

    tq = ATTN_ROWS
    out_b = pl.pallas_call(
        _attn_kernel,
        grid=(bsz,),
        in_specs=[pl.BlockSpec((None, MLA_HEADS, seq, MLA_QKW), lambda b: (b, 0, 0, 0)),
                  pl.BlockSpec((None, MLA_HEADS, seq, MLA_QKW), lambda b: (b, 0, 0, 0)),
                  pl.BlockSpec((None, MLA_HEADS, seq // tq, MLA_V, tq), lambda b: (b, 0, 0, 0, 0))],
        out_specs=pl.BlockSpec((None, seq, MLA_HEADS * MLA_V), lambda b: (b, 0, 0)),
        out_shape=sd((bsz, seq, MLA_HEADS * MLA_V), BF16),
        compiler_params=_params("parallel"),
        name="attn",
    )(mq, mk, mv)

    tf = FFN_ROWS
    ftok = lambda n: pl.BlockSpec((None, tf, n), lambda b, s: (b, s, 0))
    out = pl.pallas_call(
        _ffn_kernel,
        grid=(bsz, seq // tf),
        in_specs=[ftok(D_MODEL), ftok(GDN_VW), ftok(MLA_HEADS * MLA_V), ftok(PLE_DIM),
                  _const_spec((GDN_VW + MLA_HEADS * MLA_V, D_MODEL)),
                  _const_spec((1, D_MODEL)), _const_spec((1, D_MODEL)),
                  _const_spec(wup.shape), _const_spec(ffn_conv_w.shape), _const_spec((1, 2 * D_FF)),
                  _const_spec(wdn.shape), _const_spec((D_MODEL, D_MODEL)), _const_spec((1, D_MODEL)),
                  _const_spec((PLE_DIM, D_MODEL)), _const_spec((1, D_MODEL)), _const_spec((1, D_MODEL))],
        out_specs=ftok(D_MODEL),
        out_shape=sd((bsz, seq, D_MODEL), F32),
        scratch_shapes=[pltpu.VMEM(((FFN_CONV - 1) * SUBLANES, 2 * D_FF), F32),
                        pltpu.VMEM((FFN_STREAMS, D_MODEL // LANES, tf // FFN_STREAMS, LANES), F32),
                        pltpu.VMEM((FFN_STREAMS, tf // FFN_STREAMS, D_FF), BF16)],
        compiler_params=_params("parallel", "arbitrary"),
        name="ffn",
    )(x, out_a, out_b, p, w_out.astype(BF16), row(ln1_g), row(ln1_b), wup, ffn_conv_w, row(ffn_conv_b), wdn,
      ple_w_gate.astype(BF16), row(ple_b_gate), ple_w_proj.astype(BF16), row(ln2_g), row(ln2_b))
    return out


def kernel(x, p, w_in, gdn_conv_w, gdn_a_log, gdn_dt_bias, gdn_norm_g, mla_q_norm_g, mla_w_q_up,
           mla_kv_norm_g, mla_w_kv_up, w_out, ln1_g, ln1_b, ffn_w_up, ffn_conv_w, ffn_conv_b,
           ffn_w_down, ple_w_gate, ple_b_gate, ple_w_proj, ln2_g, ln2_b):
    h = x
    for i in range(DEPTH):
        h = _layer(h, p[i], w_in[i], gdn_conv_w[i], gdn_a_log[i], gdn_dt_bias[i], gdn_norm_g[i],
                   mla_q_norm_g[i], mla_w_q_up[i], mla_kv_norm_g[i], mla_w_kv_up[i], w_out[i],
                   ln1_g[i], ln1_b[i], ffn_w_up[i], ffn_conv_w[i], ffn_conv_b[i], ffn_w_down[i],
                   ple_w_gate[i], ple_b_gate[i], ple_w_proj[i], ln2_g[i], ln2_b[i])
    return h
```
